```python
import math
import jax, jax.numpy as jnp
from jax import lax
import numpy as np

D_MODEL = 1024
BATCH = 8
SEQ = 4096
DEPTH = 2
DEC_BATCH = 16
DEC_SEQ = 32
PAST_LEN = 4096

CHUNK = 64
POOL_WIDTH = D_MODEL // 4
POOL_WINDOWS = (2, 4, 8, 16)
POOL_GROUPS = len(POOL_WINDOWS)
POOL_GROUP_DIM = POOL_WIDTH // POOL_GROUPS
POOL_STATE = max(POOL_WINDOWS) - 1
CONV_DIM = D_MODEL // 4
CONV_K = 3
QK_DIM = 64
V_DIM = 2 * QK_DIM
ATTN_WIDTH = D_MODEL // 2
ATTN_HEADS = ATTN_WIDTH // V_DIM
QK_COLS = ATTN_HEADS * 2 * QK_DIM
MIX_WIDTH = POOL_WIDTH + CONV_DIM + ATTN_WIDTH
SPLIT_SIZES = (POOL_WIDTH, CONV_DIM, CONV_DIM, CONV_DIM, QK_COLS, QK_COLS, ATTN_WIDTH)
IN_COLS = sum(SPLIT_SIZES)
Q_BLOCK = 128
NUM_BUCKETS = 32
MAX_DISTANCE = 128
D_FF = 2816
ALPHA = (2 * DEPTH) ** 0.25
BETA = (8 * DEPTH) ** -0.25
LN_EPS = 1e-5
RMS_EPS = 1e-5
NEG_INF = -1e30

kernel_name = "hybrid_pool_conv_diffattn_streaming_encoder_step"


def layer_norm(x, g, b):
    xf = x.astype(jnp.float32)
    mu = jnp.mean(xf, axis=-1, keepdims=True)
    var = jnp.mean(jnp.square(xf - mu), axis=-1, keepdims=True)
    return ((xf - mu) * lax.rsqrt(var + LN_EPS) * g.astype(jnp.float32) + b.astype(jnp.float32)).astype(x.dtype)


def swiglu(x, w_in, w_out):
    gate, up = jnp.split(x @ w_in, 2, axis=-1)
    return (jax.nn.silu(gate) * up) @ w_out


def pool_mixer(u, left, pos, pool_w, pool_scale):
    b, s = u.shape[:2]
    ext = jnp.concatenate([left.astype(u.dtype), u], axis=1)
    cs = jnp.cumsum(ext.astype(jnp.float32), axis=1)
    cs = jnp.pad(cs, ((0, 0), (1, 0), (0, 0)))
    end = POOL_STATE + 1
    means = []
    for g, w in enumerate(POOL_WINDOWS):
        sl = slice(g * POOL_GROUP_DIM, (g + 1) * POOL_GROUP_DIM)
        win_sum = cs[:, end:end + s, sl] - cs[:, end - w:end - w + s, sl]
        count = jnp.minimum(w, pos + 1).astype(jnp.float32)[None, :, None]
        means.append(win_sum / count)
    mean = jnp.concatenate(means, axis=-1)
    d = (mean - u.astype(jnp.float32)).astype(u.dtype).reshape(b, s, POOL_GROUPS, POOL_GROUP_DIM)
    y = jnp.einsum("bsgc,gcd->bsgd", d, pool_w).reshape(b, s, POOL_WIDTH) * pool_scale
    return y, ext[:, -POOL_STATE:]


def short_conv(b_gate, c_gate, h, left, conv_w):
    z = c_gate * h
    s = z.shape[1]
    ext = jnp.concatenate([left.astype(z.dtype), z], axis=1)
    y = sum(conv_w[j] * ext[:, j:j + s] for j in range(CONV_K))
    return b_gate * y, ext[:, -(CONV_K - 1):]


def t5_bucket(rel):
    nb = NUM_BUCKETS // 2
    max_exact = nb // 2
    ret = (rel > 0).astype(jnp.int32) * nb
    n = jnp.abs(rel)
    nf = jnp.maximum(n, 1).astype(jnp.float32)
    large = max_exact + (jnp.log(nf / max_exact) / math.log(MAX_DISTANCE / max_exact) * (nb - max_exact)).astype(jnp.int32)
    large = jnp.minimum(large, nb - 1)
    return ret + jnp.where(n < max_exact, n, large)


def diff_attn_block(q, k, v, q_pos, k_pos, rel_bias, lam):
    bias = jnp.transpose(rel_bias[t5_bucket(k_pos[None, :] - q_pos[:, None])], (2, 0, 1)).astype(jnp.float32)
    mask = (k_pos[None, :] // CHUNK) <= (q_pos[:, None] // CHUNK)
    s = jnp.einsum("bqhcd,bkhcd->bchqk", q.astype(jnp.float32), k.astype(jnp.float32)) * (QK_DIM ** -0.5) + bias
    s = jnp.where(mask, s, NEG_INF)
    p = jax.nn.softmax(s, axis=-1)
    w = p[:, 0] - lam * p[:, 1]
    return jnp.einsum("bhqk,bkhd->bqhd", w, v.astype(jnp.float32))


def diff_attention(q, k, v, q_pos, k_pos, rel_bias, lam, lam_init, subln_g, sweep):
    b, s = q.shape[:2]
    if sweep:
        nblk = s // Q_BLOCK
        qb = jnp.moveaxis(q.reshape(b, nblk, Q_BLOCK, ATTN_HEADS, 2, QK_DIM), 1, 0)
        pb = q_pos.reshape(nblk, Q_BLOCK)
        o = lax.map(lambda a: diff_attn_block(a[0], k, v, a[1], k_pos, rel_bias, lam), (qb, pb))
        o = jnp.moveaxis(o, 0, 1).reshape(b, s, ATTN_HEADS, V_DIM)
    else:
        o = diff_attn_block(q, k, v, q_pos, k_pos, rel_bias, lam)
    o = o * lax.rsqrt(jnp.mean(jnp.square(o), axis=-1, keepdims=True) + RMS_EPS) * subln_g.astype(jnp.float32)
    o = o * (1.0 - lam_init)
    return o.reshape(b, s, ATTN_WIDTH).astype(q.dtype)


def encoder_layer(x, pos, pool_left, conv_left, past_k, past_v, lam_init,
                  ln_g, ln_b, w_ffn_in, w_ffn_out, w_in, w_out,
                  pool_w, pool_scale, conv_w, diff_lambda, subln_g, rel_bias):
    b, s = x.shape[:2]
    x = layer_norm(ALPHA * x + 0.5 * swiglu(x, w_ffn_in[0], w_ffn_out[0]), ln_g[0], ln_b[0])
    proj = x @ w_in
    parts = []
    off = 0
    for n in SPLIT_SIZES:
        parts.append(proj[..., off:off + n])
        off += n
    u_pool, b_gate, c_gate, h_conv, q, k, v = parts
    q = q.reshape(b, s, ATTN_HEADS, 2, QK_DIM)
    k = k.reshape(b, s, ATTN_HEADS, 2, QK_DIM)
    v = v.reshape(b, s, ATTN_HEADS, V_DIM)
    if past_k is None:
        kk, vv, k_pos, sweep = k, v, pos, True
    else:
        p_len = past_k.shape[1]
        kk = jnp.concatenate([past_k.reshape(b, p_len, ATTN_HEADS, 2, QK_DIM).astype(k.dtype), k], axis=1)
        vv = jnp.concatenate([past_v.astype(v.dtype), v], axis=1)
        k_pos = jnp.concatenate([jnp.arange(p_len, dtype=jnp.int32), pos])
        sweep = False
    dl = diff_lambda.astype(jnp.float32)
    lam = jnp.exp(jnp.sum(dl[0] * dl[1])) - jnp.exp(jnp.sum(dl[2] * dl[3])) + lam_init
    attn = diff_attention(q, kk, vv, pos, k_pos, rel_bias, lam, lam_init, subln_g, sweep)
    pool_out, pool_state = pool_mixer(u_pool, pool_left, pos, pool_w, pool_scale)
    conv_out, conv_state = short_conv(b_gate, c_gate, h_conv, conv_left, conv_w)
    mix = jnp.concatenate([pool_out, conv_out, attn], axis=-1) @ w_out
    x = layer_norm(ALPHA * x + mix, ln_g[1], ln_b[1])
    x = layer_norm(ALPHA * x + 0.5 * swiglu(x, w_ffn_in[1], w_ffn_out[1]), ln_g[2], ln_b[2])
    return x, k.reshape(b, s, ATTN_HEADS, 2 * QK_DIM), v, pool_state, conv_state


def setup_inputs(seed: int = 0) -> dict:
    key = jax.random.key(seed)
    ks = jax.random.split(key, 20)
    nrm = jax.random.normal
    f32 = jnp.float32
    return {
        "x_prompt": nrm(ks[0], (BATCH, SEQ, D_MODEL), f32),
        "x_sample": nrm(ks[1], (DEC_BATCH, DEC_SEQ, D_MODEL), f32),
        "cache_k": nrm(ks[2], (DEPTH, DEC_BATCH, PAST_LEN, ATTN_HEADS, 2 * QK_DIM), f32),
        "cache_v": nrm(ks[3], (DEPTH, DEC_BATCH, PAST_LEN, ATTN_HEADS, V_DIM), f32),
        "state_pool": nrm(ks[4], (DEPTH, DEC_BATCH, POOL_STATE, POOL_WIDTH), f32),
        "state_conv": nrm(ks[5], (DEPTH, DEC_BATCH, CONV_K - 1, CONV_DIM), f32),
        "ln_g": 1.0 + 0.05 * nrm(ks[6], (DEPTH, 3, D_MODEL), f32),
        "ln_b": 0.02 * nrm(ks[7], (DEPTH, 3, D_MODEL), f32),
        "w_ffn_in": nrm(ks[8], (DEPTH, 2, D_MODEL, 2 * D_FF), f32) * D_MODEL ** -0.5,
        "w_ffn_out": nrm(ks[9], (DEPTH, 2, D_FF, D_MODEL), f32) * (D_FF ** -0.5 * BETA),
        "w_in": nrm(ks[10], (DEPTH, D_MODEL, IN_COLS), f32) * D_MODEL ** -0.5,
        "w_out": nrm(ks[11], (DEPTH, MIX_WIDTH, D_MODEL), f32) * (MIX_WIDTH ** -0.5 * BETA),
        "pool_w": nrm(ks[12], (DEPTH, POOL_GROUPS, POOL_GROUP_DIM, POOL_GROUP_DIM), f32) * POOL_GROUP_DIM ** -0.5,
        "pool_scale": 1.0 + 0.1 * nrm(ks[13], (DEPTH, POOL_WIDTH), f32),
        "conv_w": nrm(ks[14], (DEPTH, CONV_K, CONV_DIM), f32) * CONV_K ** -0.5,
        "diff_lambda": 0.1 * nrm(ks[15], (DEPTH, 4, QK_DIM), f32),
        "subln_g": 1.0 + 0.05 * nrm(ks[16], (DEPTH, V_DIM), f32),
        "rel_bias": 0.5 * nrm(ks[17], (NUM_BUCKETS, ATTN_HEADS), f32),
    }


def reference(x_prompt, x_sample, cache_k, cache_v, state_pool, state_conv,
              ln_g, ln_b, w_ffn_in, w_ffn_out, w_in, w_out,
              pool_w, pool_scale, conv_w, diff_lambda, subln_g, rel_bias):
    bp, sp = x_prompt.shape[:2]
    bs, ss = x_sample.shape[:2]
    past = cache_k.shape[2]
    pos_p = jnp.arange(sp, dtype=jnp.int32)
    pos_s = past + jnp.arange(ss, dtype=jnp.int32)
    hp, hs = x_prompt, x_sample
    kp_l, vp_l, plp_l, cvp_l = [], [], [], []
    ks_l, vs_l, pls_l, cvs_l = [], [], [], []
    for l in range(DEPTH):
        lam_init = 0.8 - 0.6 * math.exp(-0.3 * l)
        lw = (ln_g[l], ln_b[l], w_ffn_in[l], w_ffn_out[l], w_in[l], w_out[l],
              pool_w[l], pool_scale[l], conv_w[l], diff_lambda[l], subln_g[l], rel_bias)
        hp, kp, vp, plp, cvp = encoder_layer(
            hp, pos_p, jnp.zeros((bp, POOL_STATE, POOL_WIDTH), hp.dtype),
            jnp.zeros((bp, CONV_K - 1, CONV_DIM), hp.dtype), None, None, lam_init, *lw)
        hs, k_s, v_s, pls, cvs = encoder_layer(
            hs, pos_s, state_pool[l], state_conv[l], cache_k[l], cache_v[l], lam_init, *lw)
        kp_l.append(kp); vp_l.append(vp); plp_l.append(plp); cvp_l.append(cvp)
        ks_l.append(k_s); vs_l.append(v_s); pls_l.append(pls); cvs_l.append(cvs)
    new_k_prompt = jnp.stack(kp_l)
    new_v_prompt = jnp.stack(vp_l)
    new_pool_prompt = jnp.stack(plp_l)
    new_conv_prompt = jnp.stack(cvp_l)
    new_k_sample = jnp.stack(ks_l)
    new_v_sample = jnp.stack(vs_l)
    new_pool_sample = jnp.stack(pls_l)
    new_conv_sample = jnp.stack(cvs_l)
    return (hp, hs, new_k_prompt, new_v_prompt, new_pool_prompt, new_conv_prompt,
            new_k_sample, new_v_sample, new_pool_sample, new_conv_sample)
```

```python
import functools
import math

import jax
import jax.numpy as jnp
from jax import lax
from jax.experimental import pallas as pl
from jax.experimental.pallas import tpu as pltpu

F32 = jnp.float32
BF16 = jnp.bfloat16

CHUNK = 64
POOL_WINDOWS = (2, 4, 8, 16)
POOL_STATE = max(POOL_WINDOWS) - 1
CONV_K = 3
QK_DIM = 64
V_DIM = 2 * QK_DIM
NUM_BUCKETS = 32
MAX_DISTANCE = 128
LN_EPS = 1e-5
RMS_EPS = 1e-5
NEG_INF = -1e30

LANES = 128
SUBLANES = 8
MXU_DIM = 256
VMEM_LIMIT = 56 * 1024 * 1024

TOKEN_TILE = 512
ATTN_TILE = 512
FF_CHUNK = MXU_DIM
POOL_HIST = 16
CONV_HIST = 8
NEAR_WINDOW = 256


def _const_spec(shape):
    nd = len(shape)
    return pl.BlockSpec(shape, lambda *_: (0,) * nd, pipeline_mode=pl.Buffered(1))


def _params(sem):
    return pltpu.CompilerParams(dimension_semantics=sem, vmem_limit_bytes=VMEM_LIMIT)


def _layer_norm(z, g, b):
    mu = jnp.mean(z, axis=-1, keepdims=True)
    zc = z - mu
    var = jnp.mean(zc * zc, axis=-1, keepdims=True)
    return zc * lax.rsqrt(var + LN_EPS) * g + b


def _swiglu(x, wi_ref, wo_ref, h_ref):
    d_ff = wo_ref.shape[0]
    xb = x.astype(BF16)
    for c in range(d_ff // FF_CHUNK):
        lo = c * FF_CHUNK
        gate = jnp.dot(xb, wi_ref[:, lo:lo + FF_CHUNK], preferred_element_type=F32)
        up = jnp.dot(xb, wi_ref[:, d_ff + lo:d_ff + lo + FF_CHUNK], preferred_element_type=F32)
        h_ref[:, lo:lo + FF_CHUNK] = (gate * jax.nn.sigmoid(gate) * up).astype(BF16)
    return jnp.dot(h_ref[...], wo_ref[...], preferred_element_type=F32)


def _ffn_kernel(alpha, x_ref, wi_ref, wo_ref, g_ref, b_ref, o_ref, h_ref):
    x = x_ref[...]
    y = _swiglu(x, wi_ref, wo_ref, h_ref)
    o_ref[...] = _layer_norm(alpha * x + 0.5 * y, g_ref[...], b_ref[...])


def _mixffn_kernel(alpha, x_ref, pc_ref, at_ref, wm_ref, g1_ref, b1_ref,
                   wi_ref, wo_ref, g2_ref, b2_ref, o_ref, h_ref):
    n_pc = pc_ref.shape[1]
    mix = jnp.dot(pc_ref[...], wm_ref[0:n_pc, :], preferred_element_type=F32)
    mix = mix + jnp.dot(at_ref[...], wm_ref[n_pc:, :], preferred_element_type=F32)
    x = _layer_norm(alpha * x_ref[...] + mix, g1_ref[...], b1_ref[...])
    y = _swiglu(x, wi_ref, wo_ref, h_ref)
    o_ref[...] = _layer_norm(alpha * x + 0.5 * y, g2_ref[...], b2_ref[...])


def _ffn_call(x, wi, wo, g, b, alpha):
    m, d = x.shape
    tm = min(TOKEN_TILE, m)
    d_ff = wo.shape[0]
    row = pl.BlockSpec((tm, d), lambda i: (i, 0))
    return pl.pallas_call(
        functools.partial(_ffn_kernel, alpha),
        out_shape=jax.ShapeDtypeStruct((m, d), F32),
        grid=(m // tm,),
        in_specs=[row, _const_spec(wi.shape), _const_spec(wo.shape),
                  _const_spec(g.shape), _const_spec(b.shape)],
        out_specs=row,
        scratch_shapes=[pltpu.VMEM((tm, d_ff), BF16)],
        compiler_params=_params(("arbitrary",)),
        name="ffn",
    )(x, wi, wo, g, b)


def _mixffn_call(x, pc, at, wm, g1, b1, wi, wo, g2, b2, alpha):
    m, d = x.shape
    tm = min(TOKEN_TILE, m)
    d_ff = wo.shape[0]
    row = pl.BlockSpec((tm, d), lambda i: (i, 0))
    half = pl.BlockSpec((tm, pc.shape[1]), lambda i: (i, 0))
    half2 = pl.BlockSpec((tm, at.shape[1]), lambda i: (i, 0))
    return pl.pallas_call(
        functools.partial(_mixffn_kernel, alpha),
        out_shape=jax.ShapeDtypeStruct((m, d), F32),
        grid=(m // tm,),
        in_specs=[row, half, half2, _const_spec(wm.shape), _const_spec(g1.shape), _const_spec(b1.shape),
                  _const_spec(wi.shape), _const_spec(wo.shape), _const_spec(g2.shape), _const_spec(b2.shape)],
        out_specs=row,
        scratch_shapes=[pltpu.VMEM((tm, d_ff), BF16)],
        compiler_params=_params(("arbitrary",)),
        name="mixffn",
    )(x, pc, at, wm, g1, b1, wi, wo, g2, b2)


def _inproj_kernel(pos0, x_ref, w_ref, pst_ref, cst_ref, pw_ref, ps_ref, cw_ref,
                   q_ref, k_ref, v_ref, pc_ref, pt_ref, ct_ref, pext_ref, cext_ref):
    nb, rows, d = x_ref.shape
    pool_w = pst_ref.shape[2]
    conv_w = cst_ref.shape[2]
    attn_w = q_ref.shape[2]
    i = pl.program_id(1)

    xb = x_ref[...].reshape(nb * rows, d).astype(BF16)

    def proj(lo, n):
        return jnp.dot(xb, w_ref[:, lo:lo + n], preferred_element_type=F32)

    off = 0
    u = proj(off, pool_w); off += pool_w
    b_gate = proj(off, conv_w); off += conv_w
    c_gate = proj(off, conv_w); off += conv_w
    h_conv = proj(off, conv_w); off += conv_w
    q = proj(off, attn_w); off += attn_w
    q_ref[...] = (q * (QK_DIM ** -0.5)).astype(BF16).reshape(nb, rows, attn_w)
    k_ref[...] = proj(off, attn_w).reshape(k_ref.shape); off += attn_w
    v_ref[...] = proj(off, attn_w).reshape(v_ref.shape)

    @pl.when(i == 0)
    def _():
        pext_ref[:, 0:POOL_HIST, :] = pst_ref[...]
        cext_ref[:, 0:CONV_HIST, :] = cst_ref[...]

    @pl.when(i > 0)
    def _():
        pext_ref[:, 0:POOL_HIST, :] = pext_ref[:, rows:rows + POOL_HIST, :]
        cext_ref[:, 0:CONV_HIST, :] = cext_ref[:, rows:rows + CONV_HIST, :]

    pext_ref[:, POOL_HIST:, :] = u.reshape(nb, rows, pool_w)
    cext_ref[:, CONV_HIST:, :] = (c_gate * h_conv).reshape(nb, rows, conv_w)

    group = pool_w // len(POOL_WINDOWS)
    lane = lax.broadcasted_iota(jnp.int32, (1, pool_w), 1)
    pos = pos0 + i * rows + lax.broadcasted_iota(jnp.int32, (rows, 1), 0)
    cnt = [jnp.minimum(w, pos + 1).astype(F32) for w in POOL_WINDOWS]
    count = jnp.where(lane < group, cnt[0],
                      jnp.where(lane < 2 * group, cnt[1],
                                jnp.where(lane < 3 * group, cnt[2], cnt[3])))
    for n in range(nb):
        e = pext_ref[n]
        s2 = e + pltpu.roll(e, 1, 0)
        s4 = s2 + pltpu.roll(s2, 2, 0)
        s8 = s4 + pltpu.roll(s4, 4, 0)
        s16 = s8 + pltpu.roll(s8, 8, 0)
        win = jnp.where(lane < group, s2,
                        jnp.where(lane < 2 * group, s4,
                                  jnp.where(lane < 3 * group, s8, s16)))[POOL_HIST:]
        diff = (win / count - e[POOL_HIST:]).astype(BF16)
        pooled = jnp.dot(diff, pw_ref[...], preferred_element_type=F32) * ps_ref[...]
        pc_ref[n, :, 0:pool_w] = pooled.astype(BF16)

        z = cext_ref[n]
        conv = cw_ref[2:3, :] * z + cw_ref[1:2, :] * pltpu.roll(z, 1, 0) + cw_ref[0:1, :] * pltpu.roll(z, 2, 0)
        gated = b_gate[n * rows:(n + 1) * rows] * conv[CONV_HIST:]
        pc_ref[n, :, pool_w:pool_w + conv_w] = gated.astype(BF16)

    pt_ref[...] = pext_ref[:, rows:rows + POOL_HIST, :]
    ct_ref[...] = cext_ref[:, rows:rows + CONV_HIST, :]


def _inproj_call(x, w_in, pool_state, conv_state, pool_w_bd, pool_scale, conv_w, k_all, v_all,
                 layer, depth, pos0, rows):
    bsz, s, d = x.shape
    nb = bsz if rows == s and bsz * s <= TOKEN_TILE else 1
    attn_w = (w_in.shape[1] - pool_state.shape[2] - 3 * conv_state.shape[2]) // 3
    pool_w = pool_state.shape[2]
    conv_dim = conv_state.shape[2]
    grid = (bsz // nb, s // rows)

    def tile(width):
        return pl.BlockSpec((nb, rows, width), lambda b, i: (b, i, 0))

    kv_spec = pl.BlockSpec((1, nb, rows, attn_w), lambda b, i: (layer, b, i, 0))
    in_specs = [tile(d), _const_spec(w_in.shape),
                pl.BlockSpec((nb, POOL_HIST, pool_w), lambda b, i: (b, 0, 0)),
                pl.BlockSpec((nb, CONV_HIST, conv_dim), lambda b, i: (b, 0, 0)),
                _const_spec(pool_w_bd.shape), _const_spec(pool_scale.shape), _const_spec(conv_w.shape)]
    args = [x, w_in, pool_state, conv_state, pool_w_bd, pool_scale, conv_w]
    aliases = {}
    if k_all is not None:
        in_specs += [pl.BlockSpec(memory_space=pl.ANY), pl.BlockSpec(memory_space=pl.ANY)]
        aliases = {len(args): 1, len(args) + 1: 2}
        args += [k_all, v_all]
    kv_shape = jax.ShapeDtypeStruct((depth, bsz, s, attn_w), F32)
    out_shape = (jax.ShapeDtypeStruct((bsz, s, attn_w), BF16), kv_shape, kv_shape,
                 jax.ShapeDtypeStruct((bsz, s, pool_w + conv_dim), BF16),
                 jax.ShapeDtypeStruct((bsz, POOL_HIST, pool_w), F32),
                 jax.ShapeDtypeStruct((bsz, CONV_HIST, conv_dim), F32))
    out_specs = (tile(attn_w), kv_spec, kv_spec, tile(pool_w + conv_dim),
                 pl.BlockSpec((nb, POOL_HIST, pool_w), lambda b, i: (b, 0, 0)),
                 pl.BlockSpec((nb, CONV_HIST, conv_dim), lambda b, i: (b, 0, 0)))

    def body(*refs):
        n_in = len(args)
        ins = refs[:7]
        outs = refs[n_in:]
        _inproj_kernel(pos0, *ins, *outs)

    return pl.pallas_call(
        body,
        out_shape=out_shape,
        grid=grid,
        in_specs=in_specs,
        out_specs=out_specs,
        scratch_shapes=[pltpu.VMEM((nb, POOL_HIST + rows, pool_w), F32),
                        pltpu.VMEM((nb, CONV_HIST + rows, conv_dim), F32)],
        input_output_aliases=aliases,
        compiler_params=_params(("arbitrary", "arbitrary")),
        name="inproj",
    )(*args)


def _bias_kernel(q0, k0, ref_bucket, rb_ref, o_ref):
    h = pl.program_id(0)
    _, nr, nc = o_ref.shape
    qp = q0 + lax.broadcasted_iota(jnp.int32, (nr, nc), 0)
    kp = k0 + lax.broadcasted_iota(jnp.int32, (nr, nc), 1)
    rel = kp - qp
    n = jnp.abs(rel)
    nb = NUM_BUCKETS // 2
    max_exact = nb // 2
    steps = nb - max_exact
    large = jnp.full((nr, nc), max_exact, jnp.int32)
    for j in range(1, steps):
        thr = math.ceil(max_exact * (MAX_DISTANCE / max_exact) ** (j / steps) - 1e-9)
        large = large + (n >= thr).astype(jnp.int32)
    bucket = jnp.where(rel > 0, nb, 0) + jnp.where(n < max_exact, n, large)
    val = jnp.zeros((nr, nc), F32)
    for bkt in range(NUM_BUCKETS):
        val = jnp.where(bucket == bkt, rb_ref[bkt, h], val)
    val = val - rb_ref[ref_bucket, h]
    visible = (kp // CHUNK) <= (qp // CHUNK)
    o_ref[0] = jnp.where(visible, val, NEG_INF)


def _bias_call(rel_bias, q0, k0, nr, nc):
    heads = rel_bias.shape[1]
    ref_bucket = NUM_BUCKETS // 2 - 1
    return pl.pallas_call(
        functools.partial(_bias_kernel, q0, k0, ref_bucket),
        out_shape=jax.ShapeDtypeStruct((heads, nr, nc), F32),
        grid=(heads,),
        in_specs=[pl.BlockSpec(memory_space=pltpu.SMEM)],
        out_specs=pl.BlockSpec((1, nr, nc), lambda h: (h, 0, 0)),
        compiler_params=_params(("arbitrary",)),
        name="relbias",
    )(rel_bias)


def _lambda_value(dl_ref, lam_init):
    dl = dl_ref[...]
    a = jnp.sum(dl[0:1] * dl[1:2], axis=-1, keepdims=True)
    b = jnp.sum(dl[2:3] * dl[3:4], axis=-1, keepdims=True)
    return jnp.exp(a) - jnp.exp(b) + lam_init


def _split_components(q):
    lane = lax.broadcasted_iota(jnp.int32, q.shape, 1)
    zero = jnp.zeros_like(q)
    return jnp.where(lane < QK_DIM, q, zero), jnp.where(lane >= QK_DIM, q, zero)


def _scores(qm, kt):
    return lax.dot_general(qm, kt, (((1,), (1,)), ((), ())), preferred_element_type=F32)


def _sub_norm(o, g, lam_init):
    o = o * lax.rsqrt(jnp.mean(o * o, axis=-1, keepdims=True) + RMS_EPS) * g
    return o * (1.0 - lam_init)


def _attn_kernel(lam_init, q_ref, k_ref, v_ref, d0_ref, d1_ref, dl_ref, g_ref, o_ref,
                 kb_ref, vb_ref, m_ref, l_ref, acc_ref):
    i = pl.program_id(2)
    t = q_ref.shape[1]

    @pl.when(i == 0)
    def _():
        kb_ref[...] = k_ref[0, 0].astype(BF16)
        vb_ref[...] = v_ref[0, 0].astype(BF16)

    qms = _split_components(q_ref[0])
    m_ref[...] = jnp.full(m_ref.shape, NEG_INF, F32)
    l_ref[...] = jnp.zeros(l_ref.shape, F32)
    acc_ref[...] = jnp.zeros(acc_ref.shape, F32)

    def step(j, bias):
        off = pl.multiple_of(j * t, t)
        kt = kb_ref[pl.ds(off, t), :]
        vt = vb_ref[pl.ds(off, t), :]
        for c in range(2):
            s = _scores(qms[c], kt)
            if bias is not None:
                s = s + bias
            m_prev = m_ref[c]
            m_cur = jnp.maximum(m_prev, jnp.max(s, axis=-1, keepdims=True))
            scale = jnp.exp(m_prev - m_cur)
            p = jnp.exp(s - m_cur)
            l_ref[c] = scale * l_ref[c] + jnp.sum(p, axis=-1, keepdims=True)
            acc_ref[c] = scale * acc_ref[c] + jnp.dot(p.astype(BF16), vt, preferred_element_type=F32)
            m_ref[c] = m_cur

    def far(j, carry):
        step(j, None)
        return carry

    lax.fori_loop(0, jnp.maximum(i - 1, 0), far, 0)

    @pl.when(i > 0)
    def _():
        step(i - 1, d1_ref[0])

    step(i, d0_ref[0])

    lam = _lambda_value(dl_ref, lam_init)
    o = acc_ref[0] / l_ref[0] - lam * (acc_ref[1] / l_ref[1])
    o_ref[0] = _sub_norm(o, g_ref[...], lam_init).astype(BF16)


def _attn_call(q, k_all, v_all, d0, d1, dl, g, layer, lam_init):
    bsz, s, width = q.shape
    heads = width // V_DIM
    t = min(ATTN_TILE, s)
    kv_spec = pl.BlockSpec((1, 1, s, V_DIM), lambda b, h, i: (layer, b, 0, h))
    tile_spec = pl.BlockSpec((1, t, V_DIM), lambda b, h, i: (b, i, h))
    bias_spec = pl.BlockSpec((1, t, t), lambda b, h, i: (h, 0, 0))
    return pl.pallas_call(
        functools.partial(_attn_kernel, lam_init),
        out_shape=jax.ShapeDtypeStruct((bsz, s, width), BF16),
        grid=(bsz, heads, s // t),
        in_specs=[tile_spec, kv_spec, kv_spec, bias_spec, bias_spec,
                  pl.BlockSpec(dl.shape, lambda b, h, i: (0, 0)),
                  pl.BlockSpec(g.shape, lambda b, h, i: (0, 0))],
        out_specs=tile_spec,
        scratch_shapes=[pltpu.VMEM((s, V_DIM), BF16), pltpu.VMEM((s, V_DIM), BF16),
                        pltpu.VMEM((2, t, 1), F32), pltpu.VMEM((2, t, 1), F32),
                        pltpu.VMEM((2, t, V_DIM), F32)],
        compiler_params=_params(("arbitrary", "arbitrary", "arbitrary")),
        name="attn_prompt",
    )(q, k_all, v_all, d0, d1, dl, g)


def _attn_cache_kernel(lam_init, q_ref, ck_ref, cv_ref, nk_ref, nv_ref, bn_ref, bw_ref, dl_ref, g_ref, o_ref):
    past = ck_ref.shape[2]
    far_n = past - NEAR_WINDOW
    kb = ck_ref[0, 0].astype(BF16)
    vb = cv_ref[0, 0].astype(BF16)
    nkb = nk_ref[0, 0].astype(BF16)
    nvb = nv_ref[0, 0].astype(BF16)
    lam = _lambda_value(dl_ref, lam_init)
    outs = []
    for qm in _split_components(q_ref[0]):
        s_far = _scores(qm, kb[:far_n])
        s_near = _scores(qm, kb[far_n:]) + bn_ref[0]
        s_new = _scores(qm, nkb) + bw_ref[0]
        m = jnp.maximum(jnp.max(s_far, axis=-1, keepdims=True),
                        jnp.maximum(jnp.max(s_near, axis=-1, keepdims=True),
                                    jnp.max(s_new, axis=-1, keepdims=True)))
        p_far = jnp.exp(s_far - m)
        p_near = jnp.exp(s_near - m)
        p_new = jnp.exp(s_new - m)
        denom = (jnp.sum(p_far, axis=-1, keepdims=True) + jnp.sum(p_near, axis=-1, keepdims=True)
                 + jnp.sum(p_new, axis=-1, keepdims=True))
        acc = jnp.dot(p_far.astype(BF16), vb[:far_n], preferred_element_type=F32)
        acc = acc + jnp.dot(p_near.astype(BF16), vb[far_n:], preferred_element_type=F32)
        acc = acc + jnp.dot(p_new.astype(BF16), nvb, preferred_element_type=F32)
        outs.append(acc / denom)
    o = outs[0] - lam * outs[1]
    o_ref[0] = _sub_norm(o, g_ref[...], lam_init).astype(BF16)


def _attn_cache_call(q, cache_k, cache_v, k_all, v_all, b_near, b_new, dl, g, layer, lam_init):
    bsz, s, width = q.shape
    heads = width // V_DIM
    past = cache_k.shape[2]
    cache_spec = pl.BlockSpec((1, 1, past, V_DIM), lambda b, h: (layer, b, 0, h))
    new_spec = pl.BlockSpec((1, 1, s, V_DIM), lambda b, h: (layer, b, 0, h))
    tile_spec = pl.BlockSpec((1, s, V_DIM), lambda b, h: (b, 0, h))
    return pl.pallas_call(
        functools.partial(_attn_cache_kernel, lam_init),
        out_shape=jax.ShapeDtypeStruct((bsz, s, width), BF16),
        grid=(bsz, heads),
        in_specs=[tile_spec, cache_spec, cache_spec, new_spec, new_spec,
                  pl.BlockSpec((1,) + b_near.shape[1:], lambda b, h: (h, 0, 0)),
                  pl.BlockSpec((1,) + b_new.shape[1:], lambda b, h: (h, 0, 0)),
                  pl.BlockSpec(dl.shape, lambda b, h: (0, 0)),
                  pl.BlockSpec(g.shape, lambda b, h: (0, 0))],
        out_specs=tile_spec,
        compiler_params=_params(("arbitrary", "arbitrary")),
        name="attn_sample",
    )(q, cache_k, cache_v, k_all, v_all, b_near, b_new, dl, g)


def _block_diag(w):
    groups, c, _ = w.shape
    eye = jnp.eye(groups, dtype=w.dtype)
    return (eye[:, None, :, None] * w[:, :, None, :]).reshape(groups * c, groups * c)


def _pad_front(state, rows):
    return jnp.pad(state, ((0, 0), (rows - state.shape[1], 0), (0, 0)))


def _layer(x, lw, layer, depth, lam_init, alpha, pool_state, conv_state, k_all, v_all, pos0, attend):
    bsz, s, d = x.shape
    x1 = _ffn_call(x.reshape(bsz * s, d), lw["wi1"], lw["wo1"], lw["g0"], lw["b0"], alpha)
    rows = min(TOKEN_TILE, s)
    q, k_all, v_all, pc, ptail, ctail = _inproj_call(
        x1.reshape(bsz, s, d), lw["w_in"], pool_state, conv_state, lw["pool_w"], lw["pool_scale"], lw["conv_w"],
        k_all, v_all, layer, depth, pos0, rows)
    attn = attend(q, k_all, v_all)
    x3 = _mixffn_call(x1, pc.reshape(bsz * s, -1), attn.reshape(bsz * s, -1), lw["w_out"], lw["g1"], lw["b1"],
                      lw["wi2"], lw["wo2"], lw["g2"], lw["b2"], alpha)
    return x3.reshape(bsz, s, d), k_all, v_all, ptail, ctail


def kernel(x_prompt, x_sample, cache_k, cache_v, state_pool, state_conv, ln_g, ln_b, w_ffn_in, w_ffn_out,
           w_in, w_out, pool_w, pool_scale, conv_w, diff_lambda, subln_g, rel_bias):
    depth = w_in.shape[0]
    bp, sp, d = x_prompt.shape
    bs, ss, _ = x_sample.shape
    past = cache_k.shape[2]
    heads = cache_k.shape[3]
    alpha = (2 * depth) ** 0.25
    t = min(ATTN_TILE, sp)

    d0 = _bias_call(rel_bias, 0, 0, t, t)
    d1 = _bias_call(rel_bias, t, 0, t, t)
    b_near = _bias_call(rel_bias, past, past - NEAR_WINDOW, ss, NEAR_WINDOW)
    b_new = _bias_call(rel_bias, past, past, ss, ss)

    ck = cache_k.reshape(depth, bs, past, heads * V_DIM)
    cv = cache_v.reshape(depth, bs, past, heads * V_DIM)
    zeros_pool = jnp.zeros((bp, POOL_HIST, state_pool.shape[-1]), F32)
    zeros_conv = jnp.zeros((bp, CONV_HIST, state_conv.shape[-1]), F32)

    hp, hs = x_prompt, x_sample
    kp = vp = ks = vs = None
    tails = {"pp": [], "cp": [], "ps": [], "cs": []}
    for l in range(depth):
        lam_init = 0.8 - 0.6 * math.exp(-0.3 * l)
        lw = {
            "wi1": w_ffn_in[l, 0].astype(BF16), "wo1": w_ffn_out[l, 0].astype(BF16),
            "wi2": w_ffn_in[l, 1].astype(BF16), "wo2": w_ffn_out[l, 1].astype(BF16),
            "w_in": w_in[l].astype(BF16), "w_out": w_out[l].astype(BF16),
            "pool_w": _block_diag(pool_w[l]).astype(BF16),
            "pool_scale": pool_scale[l][None, :], "conv_w": conv_w[l],
            "g0": ln_g[l, 0][None, :], "b0": ln_b[l, 0][None, :],
            "g1": ln_g[l, 1][None, :], "b1": ln_b[l, 1][None, :],
            "g2": ln_g[l, 2][None, :], "b2": ln_b[l, 2][None, :],
        }
        dl = diff_lambda[l]
        g = subln_g[l][None, :]

        def attend_prompt(q, k_all, v_all, l=l, lam_init=lam_init, dl=dl, g=g):
            return _attn_call(q, k_all, v_all, d0, d1, dl, g, l, lam_init)

        def attend_sample(q, k_all, v_all, l=l, lam_init=lam_init, dl=dl, g=g):
            return _attn_cache_call(q, ck, cv, k_all, v_all, b_near, b_new, dl, g, l, lam_init)

        hp, kp, vp, pt, ct = _layer(hp, lw, l, depth, lam_init, alpha, zeros_pool, zeros_conv, kp, vp, 0,
                                    attend_prompt)
        tails["pp"].append(pt)
        tails["cp"].append(ct)
        hs, ks, vs, pt, ct = _layer(hs, lw, l, depth, lam_init, alpha, _pad_front(state_pool[l], POOL_HIST),
                                    _pad_front(state_conv[l], CONV_HIST), ks, vs, past, attend_sample)
        tails["ps"].append(pt)
        tails["cs"].append(ct)

    def states(parts, keep):
        return jnp.stack(parts)[:, :, -keep:, :]

    return (hp, hs,
            kp.reshape(depth, bp, sp, heads, V_DIM), vp.reshape(depth, bp, sp, heads, V_DIM),
            states(tails["pp"], POOL_STATE), states(tails["cp"], CONV_K - 1),
            ks.reshape(depth, bs, ss, heads, V_DIM), vs.reshape(depth, bs, ss, heads, V_DIM),
            states(tails["ps"], POOL_STATE), states(tails["cs"], CONV_K - 1))
```

```python
import functools
import math

import jax
import jax.numpy as jnp
from jax import lax
from jax.experimental import pallas as pl
from jax.experimental.pallas import tpu as pltpu

F32 = jnp.float32
BF16 = jnp.bfloat16

CHUNK = 64
POOL_WINDOWS = (2, 4, 8, 16)
POOL_STATE = max(POOL_WINDOWS) - 1
CONV_K = 3
QK_DIM = 64
V_DIM = 2 * QK_DIM
NUM_BUCKETS = 32
MAX_DISTANCE = 128
LN_EPS = 1e-5
RMS_EPS = 1e-5
NEG_INF = -1e30
LOG2E = math.log2(math.e)

LANES = 128
SUBLANES = 8
MXU_DIM = 256
VMEM_LIMIT = 56 * 1024 * 1024

TOKEN_TILE = 512
ATTN_TILE = 512
FF_CHUNK = MXU_DIM
POOL_HIST = 16
CONV_HIST = 8
NEAR_WINDOW = 256


def _const_spec(shape):
    nd = len(shape)
    return pl.BlockSpec(shape, lambda *_: (0,) * nd, pipeline_mode=pl.Buffered(1))


def _params(sem):
    return pltpu.CompilerParams(dimension_semantics=sem, vmem_limit_bytes=VMEM_LIMIT)


def _layer_norm(z, g, b):
    mu = jnp.mean(z, axis=-1, keepdims=True)
    zc = z - mu
    var = jnp.mean(zc * zc, axis=-1, keepdims=True)
    return zc * lax.rsqrt(var + LN_EPS) * g + b


def _swiglu(x, wi_ref, wo_ref, h_ref):
    d_ff = wo_ref.shape[0]
    xb = x.astype(BF16)
    for c in range(d_ff // FF_CHUNK):
        lo = c * FF_CHUNK
        gate = jnp.dot(xb, wi_ref[:, lo:lo + FF_CHUNK], preferred_element_type=F32)
        up = jnp.dot(xb, wi_ref[:, d_ff + lo:d_ff + lo + FF_CHUNK], preferred_element_type=F32)
        h_ref[:, lo:lo + FF_CHUNK] = (gate * jax.nn.sigmoid(gate) * up).astype(BF16)
    return jnp.dot(h_ref[...], wo_ref[...], preferred_element_type=F32)


def _ffn_kernel(alpha, x_ref, wi_ref, wo_ref, g_ref, b_ref, o_ref, h_ref):
    x = x_ref[...]
    y = _swiglu(x, wi_ref, wo_ref, h_ref)
    o_ref[...] = _layer_norm(alpha * x + 0.5 * y, g_ref[...], b_ref[...])


def _mixffn_kernel(alpha, x_ref, pc_ref, at_ref, wm_ref, g1_ref, b1_ref,
                   wi_ref, wo_ref, g2_ref, b2_ref, o_ref, h_ref):
    n_pc = pc_ref.shape[1]
    mix = jnp.dot(pc_ref[...], wm_ref[0:n_pc, :], preferred_element_type=F32)
    mix = mix + jnp.dot(at_ref[...], wm_ref[n_pc:, :], preferred_element_type=F32)
    x = _layer_norm(alpha * x_ref[...] + mix, g1_ref[...], b1_ref[...])
    y = _swiglu(x, wi_ref, wo_ref, h_ref)
    o_ref[...] = _layer_norm(alpha * x + 0.5 * y, g2_ref[...], b2_ref[...])


def _ffn_call(x, wi, wo, g, b, alpha):
    m, d = x.shape
    tm = min(TOKEN_TILE, m)
    d_ff = wo.shape[0]
    row = pl.BlockSpec((tm, d), lambda i: (i, 0))
    return pl.pallas_call(
        functools.partial(_ffn_kernel, alpha),
        out_shape=jax.ShapeDtypeStruct((m, d), F32),
        grid=(m // tm,),
        in_specs=[row, _const_spec(wi.shape), _const_spec(wo.shape),
                  _const_spec(g.shape), _const_spec(b.shape)],
        out_specs=row,
        scratch_shapes=[pltpu.VMEM((tm, d_ff), BF16)],
        compiler_params=_params(("arbitrary",)),
        name="ffn",
    )(x, wi, wo, g, b)


def _mixffn_call(x, pc, at, wm, g1, b1, wi, wo, g2, b2, alpha):
    m, d = x.shape
    tm = min(TOKEN_TILE, m)
    d_ff = wo.shape[0]
    row = pl.BlockSpec((tm, d), lambda i: (i, 0))
    half = pl.BlockSpec((tm, pc.shape[1]), lambda i: (i, 0))
    half2 = pl.BlockSpec((tm, at.shape[1]), lambda i: (i, 0))
    return pl.pallas_call(
        functools.partial(_mixffn_kernel, alpha),
        out_shape=jax.ShapeDtypeStruct((m, d), F32),
        grid=(m // tm,),
        in_specs=[row, half, half2, _const_spec(wm.shape), _const_spec(g1.shape), _const_spec(b1.shape),
                  _const_spec(wi.shape), _const_spec(wo.shape), _const_spec(g2.shape), _const_spec(b2.shape)],
        out_specs=row,
        scratch_shapes=[pltpu.VMEM((tm, d_ff), BF16)],
        compiler_params=_params(("arbitrary",)),
        name="mixffn",
    )(x, pc, at, wm, g1, b1, wi, wo, g2, b2)


def _inproj_kernel(pos0, x_ref, w_ref, pst_ref, cst_ref, pw_ref, ps_ref, cw_ref,
                   q_ref, kb_ref, vb_ref, k_ref, v_ref, pc_ref, pt_ref, ct_ref, pext_ref, cext_ref):
    nb, rows, d = x_ref.shape
    pool_w = pst_ref.shape[2]
    conv_w = cst_ref.shape[2]
    attn_w = q_ref.shape[2]
    i = pl.program_id(1)

    xb = x_ref[...].reshape(nb * rows, d).astype(BF16)

    def proj(lo, n):
        return jnp.dot(xb, w_ref[:, lo:lo + n], preferred_element_type=F32)

    off = 0
    u = proj(off, pool_w); off += pool_w
    b_gate = proj(off, conv_w); off += conv_w
    c_gate = proj(off, conv_w); off += conv_w
    h_conv = proj(off, conv_w); off += conv_w
    q = proj(off, attn_w); off += attn_w
    q_ref[...] = (q * (QK_DIM ** -0.5 * LOG2E)).astype(BF16).reshape(nb, rows, attn_w)
    heads = attn_w // V_DIM
    for dense_ref, out_ref in ((kb_ref, k_ref), (vb_ref, v_ref)):
        kv = proj(off, attn_w); off += attn_w
        dense_ref[...] = kv.astype(BF16).reshape(nb, rows, attn_w)
        for n in range(nb):
            for hd in range(heads):
                out_ref[0, n, pl.ds(hd, rows, stride=heads), :] = kv[n * rows:(n + 1) * rows,
                                                                     hd * V_DIM:(hd + 1) * V_DIM]

    @pl.when(i == 0)
    def _():
        pext_ref[:, 0:POOL_HIST, :] = pst_ref[...]
        cext_ref[:, 0:CONV_HIST, :] = cst_ref[...]

    @pl.when(i > 0)
    def _():
        pext_ref[:, 0:POOL_HIST, :] = pext_ref[:, rows:rows + POOL_HIST, :]
        cext_ref[:, 0:CONV_HIST, :] = cext_ref[:, rows:rows + CONV_HIST, :]

    pext_ref[:, POOL_HIST:, :] = u.reshape(nb, rows, pool_w)
    cext_ref[:, CONV_HIST:, :] = (c_gate * h_conv).reshape(nb, rows, conv_w)

    group = pool_w // len(POOL_WINDOWS)
    lane = lax.broadcasted_iota(jnp.int32, (1, pool_w), 1)
    pos = pos0 + i * rows + lax.broadcasted_iota(jnp.int32, (rows, 1), 0)
    cnt = [jnp.minimum(w, pos + 1).astype(F32) for w in POOL_WINDOWS]
    count = jnp.where(lane < group, cnt[0],
                      jnp.where(lane < 2 * group, cnt[1],
                                jnp.where(lane < 3 * group, cnt[2], cnt[3])))
    for n in range(nb):
        e = pext_ref[n]
        s2 = e + pltpu.roll(e, 1, 0)
        s4 = s2 + pltpu.roll(s2, 2, 0)
        s8 = s4 + pltpu.roll(s4, 4, 0)
        s16 = s8 + pltpu.roll(s8, 8, 0)
        win = jnp.where(lane < group, s2,
                        jnp.where(lane < 2 * group, s4,
                                  jnp.where(lane < 3 * group, s8, s16)))[POOL_HIST:]
        diff = (win / count - e[POOL_HIST:]).astype(BF16)
        pooled = jnp.dot(diff, pw_ref[...], preferred_element_type=F32) * ps_ref[...]
        pc_ref[n, :, 0:pool_w] = pooled.astype(BF16)

        z = cext_ref[n]
        conv = cw_ref[2:3, :] * z + cw_ref[1:2, :] * pltpu.roll(z, 1, 0) + cw_ref[0:1, :] * pltpu.roll(z, 2, 0)
        gated = b_gate[n * rows:(n + 1) * rows] * conv[CONV_HIST:]
        pc_ref[n, :, pool_w:pool_w + conv_w] = gated.astype(BF16)

    pt_ref[...] = pext_ref[:, rows:rows + POOL_HIST, :]
    ct_ref[...] = cext_ref[:, rows:rows + CONV_HIST, :]


def _inproj_call(x, w_in, pool_state, conv_state, pool_w_bd, pool_scale, conv_w, k_all, v_all,
                 layer, depth, pos0, rows):
    bsz, s, d = x.shape
    nb = bsz if rows == s and bsz * s <= TOKEN_TILE else 1
    attn_w = (w_in.shape[1] - pool_state.shape[2] - 3 * conv_state.shape[2]) // 3
    heads = attn_w // V_DIM
    pool_w = pool_state.shape[2]
    conv_dim = conv_state.shape[2]
    grid = (bsz // nb, s // rows)

    def tile(width):
        return pl.BlockSpec((nb, rows, width), lambda b, i: (b, i, 0))

    kv_spec = pl.BlockSpec((1, nb, rows * heads, V_DIM), lambda b, i: (layer, b, i, 0))
    in_specs = [tile(d), _const_spec(w_in.shape),
                pl.BlockSpec((nb, POOL_HIST, pool_w), lambda b, i: (b, 0, 0)),
                pl.BlockSpec((nb, CONV_HIST, conv_dim), lambda b, i: (b, 0, 0)),
                _const_spec(pool_w_bd.shape), _const_spec(pool_scale.shape), _const_spec(conv_w.shape)]
    args = [x, w_in, pool_state, conv_state, pool_w_bd, pool_scale, conv_w]
    aliases = {}
    if k_all is not None:
        in_specs += [pl.BlockSpec(memory_space=pl.ANY), pl.BlockSpec(memory_space=pl.ANY)]
        aliases = {len(args): 3, len(args) + 1: 4}
        args += [k_all, v_all]
    kv_shape = jax.ShapeDtypeStruct((depth, bsz, s * heads, V_DIM), F32)
    dense = jax.ShapeDtypeStruct((bsz, s, attn_w), BF16)
    out_shape = (dense, dense, dense, kv_shape, kv_shape,
                 jax.ShapeDtypeStruct((bsz, s, pool_w + conv_dim), BF16),
                 jax.ShapeDtypeStruct((bsz, POOL_HIST, pool_w), F32),
                 jax.ShapeDtypeStruct((bsz, CONV_HIST, conv_dim), F32))
    out_specs = (tile(attn_w), tile(attn_w), tile(attn_w), kv_spec, kv_spec, tile(pool_w + conv_dim),
                 pl.BlockSpec((nb, POOL_HIST, pool_w), lambda b, i: (b, 0, 0)),
                 pl.BlockSpec((nb, CONV_HIST, conv_dim), lambda b, i: (b, 0, 0)))

    def body(*refs):
        n_in = len(args)
        ins = refs[:7]
        outs = refs[n_in:]
        _inproj_kernel(pos0, *ins, *outs)

    return pl.pallas_call(
        body,
        out_shape=out_shape,
        grid=grid,
        in_specs=in_specs,
        out_specs=out_specs,
        scratch_shapes=[pltpu.VMEM((nb, POOL_HIST + rows, pool_w), F32),
                        pltpu.VMEM((nb, CONV_HIST + rows, conv_dim), F32)],
        input_output_aliases=aliases,
        compiler_params=_params(("arbitrary", "arbitrary")),
        name="inproj",
    )(*args)


def _bias_kernel(q0, k0, ref_bucket, rb_ref, o_ref):
    h = pl.program_id(0)
    _, nr, nc = o_ref.shape
    qp = q0 + lax.broadcasted_iota(jnp.int32, (nr, nc), 0)
    kp = k0 + lax.broadcasted_iota(jnp.int32, (nr, nc), 1)
    rel = kp - qp
    n = jnp.abs(rel)
    nb = NUM_BUCKETS // 2
    max_exact = nb // 2
    steps = nb - max_exact
    large = jnp.full((nr, nc), max_exact, jnp.int32)
    for j in range(1, steps):
        thr = math.ceil(max_exact * (MAX_DISTANCE / max_exact) ** (j / steps) - 1e-9)
        large = large + (n >= thr).astype(jnp.int32)
    bucket = jnp.where(rel > 0, nb, 0) + jnp.where(n < max_exact, n, large)
    val = jnp.zeros((nr, nc), F32)
    for bkt in range(NUM_BUCKETS):
        val = jnp.where(bucket == bkt, rb_ref[bkt, h], val)
    val = (val - rb_ref[ref_bucket, h]) * LOG2E
    visible = (kp // CHUNK) <= (qp // CHUNK)
    o_ref[0] = jnp.where(visible, val, NEG_INF)


def _bias_call(rel_bias, q0, k0, nr, nc):
    heads = rel_bias.shape[1]
    ref_bucket = NUM_BUCKETS // 2 - 1
    return pl.pallas_call(
        functools.partial(_bias_kernel, q0, k0, ref_bucket),
        out_shape=jax.ShapeDtypeStruct((heads, nr, nc), F32),
        grid=(heads,),
        in_specs=[pl.BlockSpec(memory_space=pltpu.SMEM)],
        out_specs=pl.BlockSpec((1, nr, nc), lambda h: (h, 0, 0)),
        compiler_params=_params(("arbitrary",)),
        name="relbias",
    )(rel_bias)


def _lambda_value(dl_ref, lam_init):
    dl = dl_ref[...]
    a = jnp.sum(dl[0:1] * dl[1:2], axis=-1, keepdims=True)
    b = jnp.sum(dl[2:3] * dl[3:4], axis=-1, keepdims=True)
    return jnp.exp(a) - jnp.exp(b) + lam_init


def _split_components(q):
    lane = lax.broadcasted_iota(jnp.int32, q.shape, 1)
    zero = jnp.zeros_like(q)
    return jnp.where(lane < QK_DIM, q, zero), jnp.where(lane >= QK_DIM, q, zero)


def _scores(qm, kt):
    return lax.dot_general(qm, kt, (((1,), (1,)), ((), ())), preferred_element_type=F32)


def _sub_norm(o, g, lam_init):
    o = o * lax.rsqrt(jnp.mean(o * o, axis=-1, keepdims=True) + RMS_EPS) * g
    return o * (1.0 - lam_init)


def _lane_blocks(s):
    return [s[:, n * LANES:(n + 1) * LANES] for n in range(s.shape[1] // LANES)]


def _tree(op, xs):
    while len(xs) > 1:
        xs = [op(xs[n], xs[n + 1]) if n + 1 < len(xs) else xs[n] for n in range(0, len(xs), 2)]
    return xs[0]


def _attn_kernel(lam_init, q_ref, k_ref, v_ref, d0_ref, d1_ref, dl_ref, g_ref, o_ref, m_ref, l_ref, acc_ref):
    i = pl.program_id(2)
    t = q_ref.shape[1]
    qms = _split_components(q_ref[0])
    m_ref[...] = jnp.full(m_ref.shape, NEG_INF, F32)
    l_ref[...] = jnp.zeros(l_ref.shape, F32)
    acc_ref[...] = jnp.zeros(acc_ref.shape, F32)

    def step(j, bias):
        off = pl.multiple_of(j * t, t)
        kt = k_ref[0, pl.ds(off, t), :]
        vt = v_ref[0, pl.ds(off, t), :]
        scores = [_scores(qm, kt) for qm in qms]
        if bias is not None:
            scores = [s + bias for s in scores]
        for c, s in enumerate(scores):
            blocks = _lane_blocks(s)
            m_prev = m_ref[c]
            m_cur = jnp.maximum(m_prev, jnp.max(_tree(jnp.maximum, blocks), axis=-1, keepdims=True))
            scale = jnp.exp2(m_prev - m_cur)
            ps = [jnp.exp2(blk - m_cur) for blk in blocks]
            l_ref[c] = scale * l_ref[c] + _tree(jnp.add, ps)
            p = jnp.concatenate([x.astype(BF16) for x in ps], axis=1)
            acc_ref[c] = scale * acc_ref[c] + jnp.dot(p, vt, preferred_element_type=F32)
            m_ref[c] = m_cur

    def far(j, carry):
        step(j, None)
        return carry

    lax.fori_loop(0, jnp.maximum(i - 1, 0), far, 0)

    @pl.when(i > 0)
    def _():
        step(i - 1, d1_ref[0])

    step(i, d0_ref[0])

    lam = _lambda_value(dl_ref, lam_init)
    outs = [acc_ref[c] / jnp.sum(l_ref[c], axis=-1, keepdims=True) for c in range(2)]
    o_ref[0] = _sub_norm(outs[0] - lam * outs[1], g_ref[...], lam_init).astype(BF16)


def _attn_call(q, k, v, d0, d1, dl, g, lam_init):
    bsz, s, width = q.shape
    heads = width // V_DIM
    t = min(ATTN_TILE, s)
    kv_spec = pl.BlockSpec((1, s, V_DIM), lambda b, h, i: (b, 0, h))
    tile_spec = pl.BlockSpec((1, t, V_DIM), lambda b, h, i: (b, i, h))
    bias_spec = pl.BlockSpec((1, t, t), lambda b, h, i: (h, 0, 0))
    return pl.pallas_call(
        functools.partial(_attn_kernel, lam_init),
        out_shape=jax.ShapeDtypeStruct((bsz, s, width), BF16),
        grid=(bsz, heads, s // t),
        in_specs=[tile_spec, kv_spec, kv_spec, bias_spec, bias_spec,
                  pl.BlockSpec(dl.shape, lambda b, h, i: (0, 0)),
                  pl.BlockSpec(g.shape, lambda b, h, i: (0, 0))],
        out_specs=tile_spec,
        scratch_shapes=[pltpu.VMEM((2, t, LANES), F32), pltpu.VMEM((2, t, LANES), F32),
                        pltpu.VMEM((2, t, V_DIM), F32)],
        compiler_params=_params(("arbitrary", "arbitrary", "arbitrary")),
        name="attn_prompt",
    )(q, k, v, d0, d1, dl, g)


def _attn_cache_kernel(lam_init, heads, q_ref, ck_ref, cv_ref, nk_ref, nv_ref, bn_ref, bw_ref, dl_ref, g_ref,
                       o_ref):
    past = ck_ref.shape[2] // heads
    far_n = past - NEAR_WINDOW
    lam = _lambda_value(dl_ref, lam_init)
    for hd in range(heads):
        lanes = slice(hd * V_DIM, (hd + 1) * V_DIM)
        kb = ck_ref[0, 0, pl.ds(hd, past, stride=heads), :].astype(BF16)
        vb = cv_ref[0, 0, pl.ds(hd, past, stride=heads), :].astype(BF16)
        nkb = nk_ref[0, :, lanes]
        nvb = nv_ref[0, :, lanes]
        outs = []
        for qm in _split_components(q_ref[0, :, lanes]):
            s_far = _scores(qm, kb[:far_n])
            s_near = _scores(qm, kb[far_n:]) + bn_ref[hd]
            s_new = _scores(qm, nkb) + bw_ref[hd]
            m = jnp.maximum(jnp.max(s_far, axis=-1, keepdims=True),
                            jnp.maximum(jnp.max(s_near, axis=-1, keepdims=True),
                                        jnp.max(s_new, axis=-1, keepdims=True)))
            p_far = jnp.exp2(s_far - m)
            p_near = jnp.exp2(s_near - m)
            p_new = jnp.exp2(s_new - m)
            denom = (jnp.sum(p_far, axis=-1, keepdims=True) + jnp.sum(p_near, axis=-1, keepdims=True)
                     + jnp.sum(p_new, axis=-1, keepdims=True))
            acc = jnp.dot(p_far.astype(BF16), vb[:far_n], preferred_element_type=F32)
            acc = acc + jnp.dot(p_near.astype(BF16), vb[far_n:], preferred_element_type=F32)
            acc = acc + jnp.dot(p_new.astype(BF16), nvb, preferred_element_type=F32)
            outs.append(acc / denom)
        o = outs[0] - lam * outs[1]
        o_ref[0, :, lanes] = _sub_norm(o, g_ref[...], lam_init).astype(BF16)


def _attn_cache_call(q, cache_k, cache_v, k_new, v_new, b_near, b_new, dl, g, layer, lam_init):
    bsz, s, width = q.shape
    heads = width // V_DIM
    cache_spec = pl.BlockSpec((1, 1) + cache_k.shape[2:], lambda b: (layer, b, 0, 0))
    tile_spec = pl.BlockSpec((1, s, width), lambda b: (b, 0, 0))
    return pl.pallas_call(
        functools.partial(_attn_cache_kernel, lam_init, heads),
        out_shape=jax.ShapeDtypeStruct((bsz, s, width), BF16),
        grid=(bsz,),
        in_specs=[tile_spec, cache_spec, cache_spec, tile_spec, tile_spec,
                  _const_spec(b_near.shape), _const_spec(b_new.shape), _const_spec(dl.shape), _const_spec(g.shape)],
        out_specs=tile_spec,
        compiler_params=_params(("arbitrary",)),
        name="attn_sample",
    )(q, cache_k, cache_v, k_new, v_new, b_near, b_new, dl, g)


def _block_diag(w):
    groups, c, _ = w.shape
    eye = jnp.eye(groups, dtype=w.dtype)
    return (eye[:, None, :, None] * w[:, :, None, :]).reshape(groups * c, groups * c)


def _pad_front(state, rows):
    return jnp.pad(state, ((0, 0), (rows - state.shape[1], 0), (0, 0)))


def _layer(x, lw, layer, depth, lam_init, alpha, pool_state, conv_state, k_all, v_all, pos0, attend):
    bsz, s, d = x.shape
    x1 = _ffn_call(x.reshape(bsz * s, d), lw["wi1"], lw["wo1"], lw["g0"], lw["b0"], alpha)
    rows = min(TOKEN_TILE, s)
    q, kb, vb, k_all, v_all, pc, ptail, ctail = _inproj_call(
        x1.reshape(bsz, s, d), lw["w_in"], pool_state, conv_state, lw["pool_w"], lw["pool_scale"], lw["conv_w"],
        k_all, v_all, layer, depth, pos0, rows)
    attn = attend(q, kb, vb)
    x3 = _mixffn_call(x1, pc.reshape(bsz * s, -1), attn.reshape(bsz * s, -1), lw["w_out"], lw["g1"], lw["b1"],
                      lw["wi2"], lw["wo2"], lw["g2"], lw["b2"], alpha)
    return x3.reshape(bsz, s, d), k_all, v_all, ptail, ctail


def kernel(x_prompt, x_sample, cache_k, cache_v, state_pool, state_conv, ln_g, ln_b, w_ffn_in, w_ffn_out,
           w_in, w_out, pool_w, pool_scale, conv_w, diff_lambda, subln_g, rel_bias):
    depth = w_in.shape[0]
    bp, sp, d = x_prompt.shape
    bs, ss, _ = x_sample.shape
    past = cache_k.shape[2]
    heads = cache_k.shape[3]
    alpha = (2 * depth) ** 0.25
    t = min(ATTN_TILE, sp)

    d0 = _bias_call(rel_bias, 0, 0, t, t)
    d1 = _bias_call(rel_bias, t, 0, t, t)
    b_near = _bias_call(rel_bias, past, past - NEAR_WINDOW, ss, NEAR_WINDOW)
    b_new = _bias_call(rel_bias, past, past, ss, ss)

    ck = cache_k.reshape(depth, bs, past * heads, V_DIM)
    cv = cache_v.reshape(depth, bs, past * heads, V_DIM)
    zeros_pool = jnp.zeros((bp, POOL_HIST, state_pool.shape[-1]), F32)
    zeros_conv = jnp.zeros((bp, CONV_HIST, state_conv.shape[-1]), F32)

    hp, hs = x_prompt, x_sample
    kp = vp = ks = vs = None
    tails = {"pp": [], "cp": [], "ps": [], "cs": []}
    for l in range(depth):
        lam_init = 0.8 - 0.6 * math.exp(-0.3 * l)
        lw = {
            "wi1": w_ffn_in[l, 0].astype(BF16), "wo1": w_ffn_out[l, 0].astype(BF16),
            "wi2": w_ffn_in[l, 1].astype(BF16), "wo2": w_ffn_out[l, 1].astype(BF16),
            "w_in": w_in[l].astype(BF16), "w_out": w_out[l].astype(BF16),
            "pool_w": _block_diag(pool_w[l]).astype(BF16),
            "pool_scale": pool_scale[l][None, :], "conv_w": conv_w[l],
            "g0": ln_g[l, 0][None, :], "b0": ln_b[l, 0][None, :],
            "g1": ln_g[l, 1][None, :], "b1": ln_b[l, 1][None, :],
            "g2": ln_g[l, 2][None, :], "b2": ln_b[l, 2][None, :],
        }
        dl = diff_lambda[l]
        g = subln_g[l][None, :]

        def attend_prompt(q, kb, vb, lam_init=lam_init, dl=dl, g=g):
            return _attn_call(q, kb, vb, d0, d1, dl, g, lam_init)

        def attend_sample(q, kb, vb, l=l, lam_init=lam_init, dl=dl, g=g):
            return _attn_cache_call(q, ck, cv, kb, vb, b_near, b_new, dl, g, l, lam_init)

        hp, kp, vp, pt, ct = _layer(hp, lw, l, depth, lam_init, alpha, zeros_pool, zeros_conv, kp, vp, 0,
                                    attend_prompt)
        tails["pp"].append(pt)
        tails["cp"].append(ct)
        hs, ks, vs, pt, ct = _layer(hs, lw, l, depth, lam_init, alpha, _pad_front(state_pool[l], POOL_HIST),
                                    _pad_front(state_conv[l], CONV_HIST), ks, vs, past, attend_sample)
        tails["ps"].append(pt)
        tails["cs"].append(ct)

    def states(parts, keep):
        return jnp.stack(parts)[:, :, -keep:, :]

    return (hp, hs,
            kp.reshape(depth, bp, sp, heads, V_DIM), vp.reshape(depth, bp, sp, heads, V_DIM),
            states(tails["pp"], POOL_STATE), states(tails["cp"], CONV_K - 1),
            ks.reshape(depth, bs, ss, heads, V_DIM), vs.reshape(depth, bs, ss, heads, V_DIM),
            states(tails["ps"], POOL_STATE), states(tails["cs"], CONV_K - 1))
```

```python
import functools
import math

import jax
import jax.numpy as jnp
from jax import lax
from jax.experimental import pallas as pl
from jax.experimental.pallas import tpu as pltpu

F32 = jnp.float32
BF16 = jnp.bfloat16

CHUNK = 64
POOL_WINDOWS = (2, 4, 8, 16)
POOL_STATE = max(POOL_WINDOWS) - 1
CONV_K = 3
QK_DIM = 64
V_DIM = 2 * QK_DIM
NUM_BUCKETS = 32
MAX_DISTANCE = 128
LN_EPS = 1e-5
RMS_EPS = 1e-5
NEG_INF = -1e30
LOG2E = math.log2(math.e)

LANES = 128
SUBLANES = 8
MXU_DIM = 256
VMEM_LIMIT = 56 * 1024 * 1024

TOKEN_TILE = 512
ATTN_TILE = 512
FF_CHUNK = MXU_DIM
POOL_HIST = 16
CONV_HIST = 8
NEAR_WINDOW = 256
SUB = MAX_DISTANCE
SOFTMAX_ROWS = 64


def _const_spec(shape):
    nd = len(shape)
    return pl.BlockSpec(shape, lambda *_: (0,) * nd, pipeline_mode=pl.Buffered(1))


def _params(sem):
    return pltpu.CompilerParams(dimension_semantics=sem, vmem_limit_bytes=VMEM_LIMIT)


def _layer_norm(z, g, b):
    mu = jnp.mean(z, axis=-1, keepdims=True)
    zc = z - mu
    var = jnp.mean(zc * zc, axis=-1, keepdims=True)
    return zc * lax.rsqrt(var + LN_EPS) * g + b


def _swiglu(x, wi_ref, wo_ref, h_ref):
    d_ff = wo_ref.shape[0]
    xb = x.astype(BF16)
    for c in range(d_ff // FF_CHUNK):
        lo = c * FF_CHUNK
        gate = jnp.dot(xb, wi_ref[:, lo:lo + FF_CHUNK], preferred_element_type=F32)
        up = jnp.dot(xb, wi_ref[:, d_ff + lo:d_ff + lo + FF_CHUNK], preferred_element_type=F32)
        h_ref[:, lo:lo + FF_CHUNK] = (gate * jax.nn.sigmoid(gate) * up).astype(BF16)
    return jnp.dot(h_ref[...], wo_ref[...], preferred_element_type=F32)


def _ffn_kernel(alpha, x_ref, wi_ref, wo_ref, g_ref, b_ref, o_ref, h_ref):
    x = x_ref[...]
    y = _swiglu(x, wi_ref, wo_ref, h_ref)
    o_ref[...] = _layer_norm(alpha * x + 0.5 * y, g_ref[...], b_ref[...])


def _mixffn_kernel(alpha, x_ref, pc_ref, at_ref, wm_ref, g1_ref, b1_ref,
                   wi_ref, wo_ref, g2_ref, b2_ref, o_ref, h_ref):
    n_pc = pc_ref.shape[1]
    mix = jnp.dot(pc_ref[...], wm_ref[0:n_pc, :], preferred_element_type=F32)
    mix = mix + jnp.dot(at_ref[...], wm_ref[n_pc:, :], preferred_element_type=F32)
    x = _layer_norm(alpha * x_ref[...] + mix, g1_ref[...], b1_ref[...])
    y = _swiglu(x, wi_ref, wo_ref, h_ref)
    o_ref[...] = _layer_norm(alpha * x + 0.5 * y, g2_ref[...], b2_ref[...])


def _ffn_call(x, wi, wo, g, b, alpha):
    m, d = x.shape
    tm = min(TOKEN_TILE, m)
    d_ff = wo.shape[0]
    row = pl.BlockSpec((tm, d), lambda i: (i, 0))
    return pl.pallas_call(
        functools.partial(_ffn_kernel, alpha),
        out_shape=jax.ShapeDtypeStruct((m, d), F32),
        grid=(m // tm,),
        in_specs=[row, _const_spec(wi.shape), _const_spec(wo.shape),
                  _const_spec(g.shape), _const_spec(b.shape)],
        out_specs=row,
        scratch_shapes=[pltpu.VMEM((tm, d_ff), BF16)],
        compiler_params=_params(("arbitrary",)),
        name="ffn",
    )(x, wi, wo, g, b)


def _mixffn_call(x, pc, at, wm, g1, b1, wi, wo, g2, b2, alpha):
    m, d = x.shape
    tm = min(TOKEN_TILE, m)
    d_ff = wo.shape[0]
    row = pl.BlockSpec((tm, d), lambda i: (i, 0))
    half = pl.BlockSpec((tm, pc.shape[1]), lambda i: (i, 0))
    half2 = pl.BlockSpec((tm, at.shape[1]), lambda i: (i, 0))
    return pl.pallas_call(
        functools.partial(_mixffn_kernel, alpha),
        out_shape=jax.ShapeDtypeStruct((m, d), F32),
        grid=(m // tm,),
        in_specs=[row, half, half2, _const_spec(wm.shape), _const_spec(g1.shape), _const_spec(b1.shape),
                  _const_spec(wi.shape), _const_spec(wo.shape), _const_spec(g2.shape), _const_spec(b2.shape)],
        out_specs=row,
        scratch_shapes=[pltpu.VMEM((tm, d_ff), BF16)],
        compiler_params=_params(("arbitrary",)),
        name="mixffn",
    )(x, pc, at, wm, g1, b1, wi, wo, g2, b2)


def _inproj_kernel(pos0, x_ref, w_ref, pst_ref, cst_ref, pw_ref, ps_ref, cw_ref,
                   q_ref, kb_ref, vb_ref, k_ref, v_ref, pc_ref, pt_ref, ct_ref, pext_ref, cext_ref):
    nb, rows, d = x_ref.shape
    pool_w = pst_ref.shape[2]
    conv_w = cst_ref.shape[2]
    attn_w = q_ref.shape[2]
    i = pl.program_id(1)

    xb = x_ref[...].reshape(nb * rows, d).astype(BF16)

    def proj(lo, n):
        return jnp.dot(xb, w_ref[:, lo:lo + n], preferred_element_type=F32)

    off = 0
    u = proj(off, pool_w); off += pool_w
    b_gate = proj(off, conv_w); off += conv_w
    c_gate = proj(off, conv_w); off += conv_w
    h_conv = proj(off, conv_w); off += conv_w
    q = proj(off, attn_w); off += attn_w
    q_ref[...] = (q * (QK_DIM ** -0.5 * LOG2E)).astype(BF16).reshape(nb, rows, attn_w)
    heads = attn_w // V_DIM
    for dense_ref, out_ref in ((kb_ref, k_ref), (vb_ref, v_ref)):
        kv = proj(off, attn_w); off += attn_w
        dense_ref[...] = kv.astype(BF16).reshape(nb, rows, attn_w)
        for n in range(nb):
            for hd in range(heads):
                out_ref[0, n, pl.ds(hd, rows, stride=heads), :] = kv[n * rows:(n + 1) * rows,
                                                                     hd * V_DIM:(hd + 1) * V_DIM]

    @pl.when(i == 0)
    def _():
        pext_ref[:, 0:POOL_HIST, :] = pst_ref[...]
        cext_ref[:, 0:CONV_HIST, :] = cst_ref[...]

    @pl.when(i > 0)
    def _():
        pext_ref[:, 0:POOL_HIST, :] = pext_ref[:, rows:rows + POOL_HIST, :]
        cext_ref[:, 0:CONV_HIST, :] = cext_ref[:, rows:rows + CONV_HIST, :]

    pext_ref[:, POOL_HIST:, :] = u.reshape(nb, rows, pool_w)
    cext_ref[:, CONV_HIST:, :] = (c_gate * h_conv).reshape(nb, rows, conv_w)

    group = pool_w // len(POOL_WINDOWS)
    lane = lax.broadcasted_iota(jnp.int32, (1, pool_w), 1)
    pos = pos0 + i * rows + lax.broadcasted_iota(jnp.int32, (rows, 1), 0)
    cnt = [jnp.minimum(w, pos + 1).astype(F32) for w in POOL_WINDOWS]
    count = jnp.where(lane < group, cnt[0],
                      jnp.where(lane < 2 * group, cnt[1],
                                jnp.where(lane < 3 * group, cnt[2], cnt[3])))
    for n in range(nb):
        e = pext_ref[n]
        s2 = e + pltpu.roll(e, 1, 0)
        s4 = s2 + pltpu.roll(s2, 2, 0)
        s8 = s4 + pltpu.roll(s4, 4, 0)
        s16 = s8 + pltpu.roll(s8, 8, 0)
        win = jnp.where(lane < group, s2,
                        jnp.where(lane < 2 * group, s4,
                                  jnp.where(lane < 3 * group, s8, s16)))[POOL_HIST:]
        diff = (win / count - e[POOL_HIST:]).astype(BF16)
        pooled = jnp.dot(diff, pw_ref[...], preferred_element_type=F32) * ps_ref[...]
        pc_ref[n, :, 0:pool_w] = pooled.astype(BF16)

        z = cext_ref[n]
        conv = cw_ref[2:3, :] * z + cw_ref[1:2, :] * pltpu.roll(z, 1, 0) + cw_ref[0:1, :] * pltpu.roll(z, 2, 0)
        gated = b_gate[n * rows:(n + 1) * rows] * conv[CONV_HIST:]
        pc_ref[n, :, pool_w:pool_w + conv_w] = gated.astype(BF16)

    pt_ref[...] = pext_ref[:, rows:rows + POOL_HIST, :]
    ct_ref[...] = cext_ref[:, rows:rows + CONV_HIST, :]


def _inproj_call(x, w_in, pool_state, conv_state, pool_w_bd, pool_scale, conv_w, k_all, v_all,
                 layer, depth, pos0, rows):
    bsz, s, d = x.shape
    nb = bsz if rows == s and bsz * s <= TOKEN_TILE else 1
    attn_w = (w_in.shape[1] - pool_state.shape[2] - 3 * conv_state.shape[2]) // 3
    heads = attn_w // V_DIM
    pool_w = pool_state.shape[2]
    conv_dim = conv_state.shape[2]
    grid = (bsz // nb, s // rows)

    def tile(width):
        return pl.BlockSpec((nb, rows, width), lambda b, i: (b, i, 0))

    kv_spec = pl.BlockSpec((1, nb, rows * heads, V_DIM), lambda b, i: (layer, b, i, 0))
    in_specs = [tile(d), _const_spec(w_in.shape),
                pl.BlockSpec((nb, POOL_HIST, pool_w), lambda b, i: (b, 0, 0)),
                pl.BlockSpec((nb, CONV_HIST, conv_dim), lambda b, i: (b, 0, 0)),
                _const_spec(pool_w_bd.shape), _const_spec(pool_scale.shape), _const_spec(conv_w.shape)]
    args = [x, w_in, pool_state, conv_state, pool_w_bd, pool_scale, conv_w]
    aliases = {}
    if k_all is not None:
        in_specs += [pl.BlockSpec(memory_space=pl.ANY), pl.BlockSpec(memory_space=pl.ANY)]
        aliases = {len(args): 3, len(args) + 1: 4}
        args += [k_all, v_all]
    kv_shape = jax.ShapeDtypeStruct((depth, bsz, s * heads, V_DIM), F32)
    dense = jax.ShapeDtypeStruct((bsz, s, attn_w), BF16)
    out_shape = (dense, dense, dense, kv_shape, kv_shape,
                 jax.ShapeDtypeStruct((bsz, s, pool_w + conv_dim), BF16),
                 jax.ShapeDtypeStruct((bsz, POOL_HIST, pool_w), F32),
                 jax.ShapeDtypeStruct((bsz, CONV_HIST, conv_dim), F32))
    out_specs = (tile(attn_w), tile(attn_w), tile(attn_w), kv_spec, kv_spec, tile(pool_w + conv_dim),
                 pl.BlockSpec((nb, POOL_HIST, pool_w), lambda b, i: (b, 0, 0)),
                 pl.BlockSpec((nb, CONV_HIST, conv_dim), lambda b, i: (b, 0, 0)))

    def body(*refs):
        n_in = len(args)
        ins = refs[:7]
        outs = refs[n_in:]
        _inproj_kernel(pos0, *ins, *outs)

    return pl.pallas_call(
        body,
        out_shape=out_shape,
        grid=grid,
        in_specs=in_specs,
        out_specs=out_specs,
        scratch_shapes=[pltpu.VMEM((nb, POOL_HIST + rows, pool_w), F32),
                        pltpu.VMEM((nb, CONV_HIST + rows, conv_dim), F32)],
        input_output_aliases=aliases,
        compiler_params=_params(("arbitrary", "arbitrary")),
        name="inproj",
    )(*args)


def _bias_kernel(q0, k0, ref_bucket, rb_ref, o_ref):
    h = pl.program_id(0)
    _, nr, nc = o_ref.shape
    qp = q0 + lax.broadcasted_iota(jnp.int32, (nr, nc), 0)
    kp = k0 + lax.broadcasted_iota(jnp.int32, (nr, nc), 1)
    rel = kp - qp
    n = jnp.abs(rel)
    nb = NUM_BUCKETS // 2
    max_exact = nb // 2
    steps = nb - max_exact
    large = jnp.full((nr, nc), max_exact, jnp.int32)
    for j in range(1, steps):
        thr = math.ceil(max_exact * (MAX_DISTANCE / max_exact) ** (j / steps) - 1e-9)
        large = large + (n >= thr).astype(jnp.int32)
    bucket = jnp.where(rel > 0, nb, 0) + jnp.where(n < max_exact, n, large)
    val = jnp.zeros((nr, nc), F32)
    for bkt in range(NUM_BUCKETS):
        val = jnp.where(bucket == bkt, rb_ref[bkt, h], val)
    val = (val - rb_ref[ref_bucket, h]) * LOG2E
    visible = (kp // CHUNK) <= (qp // CHUNK)
    o_ref[0] = jnp.where(visible, val, NEG_INF)


def _bias_call(rel_bias, q0, k0, nr, nc):
    heads = rel_bias.shape[1]
    ref_bucket = NUM_BUCKETS // 2 - 1
    return pl.pallas_call(
        functools.partial(_bias_kernel, q0, k0, ref_bucket),
        out_shape=jax.ShapeDtypeStruct((heads, nr, nc), F32),
        grid=(heads,),
        in_specs=[pl.BlockSpec(memory_space=pltpu.SMEM)],
        out_specs=pl.BlockSpec((1, nr, nc), lambda h: (h, 0, 0)),
        compiler_params=_params(("arbitrary",)),
        name="relbias",
    )(rel_bias)


def _lambda_value(dl_ref, lam_init):
    dl = dl_ref[...]
    a = jnp.sum(dl[0:1] * dl[1:2], axis=-1, keepdims=True)
    b = jnp.sum(dl[2:3] * dl[3:4], axis=-1, keepdims=True)
    return jnp.exp(a) - jnp.exp(b) + lam_init


def _split_components(q):
    lane = lax.broadcasted_iota(jnp.int32, q.shape, 1)
    zero = jnp.zeros_like(q)
    return jnp.where(lane < QK_DIM, q, zero), jnp.where(lane >= QK_DIM, q, zero)


def _scores(qm, kt):
    return lax.dot_general(qm, kt, (((1,), (1,)), ((), ())), preferred_element_type=F32)


def _sub_norm(o, g, lam_init):
    o = o * lax.rsqrt(jnp.mean(o * o, axis=-1, keepdims=True) + RMS_EPS) * g
    return o * (1.0 - lam_init)


def _lane_blocks(s):
    return [s[:, n * LANES:(n + 1) * LANES] for n in range(s.shape[1] // LANES)]


def _tree(op, xs):
    while len(xs) > 1:
        xs = [op(xs[n], xs[n + 1]) if n + 1 < len(xs) else xs[n] for n in range(0, len(xs), 2)]
    return xs[0]


def _attn_kernel(lam_init, counts, pi_ref, pj_ref, q_ref, k_ref, v_ref, d0_ref, d1_ref, dl_ref, g_ref, o_ref,
                 p_ref, m_ref, l_ref, acc_ref):
    t = p_ref.shape[1]
    nq = q_ref.shape[1] // t
    nsub = t // SUB
    half = t // 2

    m_ref[...] = jnp.full(m_ref.shape, NEG_INF, F32)
    l_ref[...] = jnp.zeros(l_ref.shape, F32)
    acc_ref[...] = jnp.zeros(acc_ref.shape, F32)

    def tile_rows(ref, idx):
        return ref[0, pl.ds(pl.multiple_of(idx * t, t), t), :]

    def pair(n, kind):
        i = pi_ref[n]
        q = tile_rows(q_ref, i)
        kt = tile_rows(k_ref, pj_ref[n])
        vt = tile_rows(v_ref, pj_ref[n])
        for c, qm in enumerate(_split_components(q)):
            if kind == "diag":
                pieces = {0: _scores(qm[0:half], kt[0:half]), 1: _scores(qm[half:], kt)}
            else:
                pieces = {0: _scores(qm[0:half], kt), 1: _scores(qm[half:], kt)}
            for rb in range(t // SOFTMAX_ROWS):
                rows = slice(rb * SOFTMAX_ROWS, (rb + 1) * SOFTMAX_ROWS)
                sub_r = rb * SOFTMAX_ROWS // SUB
                in_sub = slice(rows.start - sub_r * SUB, rows.stop - sub_r * SUB)
                used = sub_r + 1 if kind == "diag" else nsub
                piece = pieces[rows.start // half]
                prow = slice(rows.start % half, rows.start % half + SOFTMAX_ROWS)
                blocks = []
                for cb in range(used):
                    cols = slice(cb * SUB, (cb + 1) * SUB)
                    blk = piece[prow, cols]
                    if kind == "diag" and cb >= sub_r - 1:
                        blk = blk + d0_ref[0, rows, cols]
                    if kind == "sub" and sub_r == 0 and cb == nsub - 1:
                        blk = blk + d1_ref[0, in_sub, :]
                    blocks.append(blk)
                m_prev = m_ref[i, c, rows, :]
                m_cur = jnp.maximum(m_prev, jnp.max(_tree(jnp.maximum, blocks), axis=-1, keepdims=True))
                scale = jnp.exp2(m_prev - m_cur)
                ps = [jnp.exp2(blk - m_cur) for blk in blocks]
                l_ref[i, c, rows, :] = scale * l_ref[i, c, rows, :] + _tree(jnp.add, ps)
                m_ref[i, c, rows, :] = m_cur
                acc_ref[i, c, rows, :] = scale * acc_ref[i, c, rows, :]
                width = half if (kind == "diag" and rows.stop <= half) else t
                pb = [x.astype(BF16) for x in ps]
                pb += [jnp.zeros((SOFTMAX_ROWS, SUB), BF16)] * (width // SUB - len(pb))
                p_ref[c, rows, 0:width] = jnp.concatenate(pb, axis=1)
            if kind == "diag":
                acc_ref[i, c, 0:half, :] += jnp.dot(p_ref[c, 0:half, 0:half], vt[0:half],
                                                    preferred_element_type=F32)
                acc_ref[i, c, half:, :] += jnp.dot(p_ref[c, half:, :], vt, preferred_element_type=F32)
            else:
                acc_ref[i, c] += jnp.dot(p_ref[c], vt, preferred_element_type=F32)

    def run(start, count, kind):
        looped = count // 2 * 2
        if looped >= 4:
            def body(it, carry):
                n0 = start + 2 * it
                pair(n0, kind)
                pair(n0 + 1, kind)
                return carry
            lax.fori_loop(0, looped // 2, body, 0)
        else:
            looped = 0
        for n in range(start + looped, start + count):
            pair(n, kind)

    start = 0
    for count, kind in zip(counts, ("far", "sub", "diag")):
        if count:
            run(start, count, kind)
        start += count

    lam = _lambda_value(dl_ref, lam_init)

    def finish(i, carry):
        outs = [acc_ref[i, c] / jnp.sum(l_ref[i, c], axis=-1, keepdims=True) for c in range(2)]
        o = _sub_norm(outs[0] - lam * outs[1], g_ref[...], lam_init)
        o_ref[0, pl.ds(pl.multiple_of(i * t, t), t), :] = o.astype(BF16)
        return carry

    lax.fori_loop(0, nq, finish, 0)


def _pair_order(nq):
    far = [(i, j) for i in range(nq) for j in range(i - 1)]
    sub = [(i, i - 1) for i in range(1, nq)]
    diag = [(i, i) for i in range(nq)]
    return far, sub, diag


def _attn_call(q, k, v, d0, d1, dl, g, lam_init):
    bsz, s, width = q.shape
    heads = width // V_DIM
    t = min(ATTN_TILE, s)
    nq = s // t
    groups = _pair_order(nq)
    pairs = [p for grp in groups for p in grp]
    pi = jnp.asarray([p[0] for p in pairs], jnp.int32)
    pj = jnp.asarray([p[1] for p in pairs], jnp.int32)
    seq_spec = pl.BlockSpec((1, s, V_DIM), lambda b, h, *_: (b, 0, h))
    grid_spec = pltpu.PrefetchScalarGridSpec(
        num_scalar_prefetch=2,
        grid=(bsz, heads),
        in_specs=[seq_spec, seq_spec, seq_spec,
                  pl.BlockSpec((1, t, t), lambda b, h, *_: (h, 0, 0)),
                  pl.BlockSpec((1, SUB, SUB), lambda b, h, *_: (h, 0, 0)),
                  pl.BlockSpec(dl.shape, lambda b, h, *_: (0, 0)),
                  pl.BlockSpec(g.shape, lambda b, h, *_: (0, 0))],
        out_specs=seq_spec,
        scratch_shapes=[pltpu.VMEM((2, t, t), BF16),
                        pltpu.VMEM((nq, 2, t, LANES), F32), pltpu.VMEM((nq, 2, t, LANES), F32),
                        pltpu.VMEM((nq, 2, t, V_DIM), F32)])
    return pl.pallas_call(
        functools.partial(_attn_kernel, lam_init, tuple(len(grp) for grp in groups)),
        out_shape=jax.ShapeDtypeStruct((bsz, s, width), BF16),
        grid_spec=grid_spec,
        compiler_params=_params(("arbitrary", "arbitrary")),
        name="attn_prompt",
    )(pi, pj, q, k, v, d0, d1, dl, g)


def _attn_cache_kernel(lam_init, heads, q_ref, ck_ref, cv_ref, nk_ref, nv_ref, bn_ref, bw_ref, dl_ref, g_ref,
                       o_ref):
    past = ck_ref.shape[2] // heads
    far_n = past - NEAR_WINDOW
    lam = _lambda_value(dl_ref, lam_init)
    for hd in range(heads):
        lanes = slice(hd * V_DIM, (hd + 1) * V_DIM)
        kb = ck_ref[0, 0, pl.ds(hd, past, stride=heads), :].astype(BF16)
        vb = cv_ref[0, 0, pl.ds(hd, past, stride=heads), :].astype(BF16)
        nkb = nk_ref[0, :, lanes]
        nvb = nv_ref[0, :, lanes]
        outs = []
        for qm in _split_components(q_ref[0, :, lanes]):
            s_far = _scores(qm, kb[:far_n])
            s_near = _scores(qm, kb[far_n:]) + bn_ref[hd]
            s_new = _scores(qm, nkb) + bw_ref[hd]
            m = jnp.maximum(jnp.max(s_far, axis=-1, keepdims=True),
                            jnp.maximum(jnp.max(s_near, axis=-1, keepdims=True),
                                        jnp.max(s_new, axis=-1, keepdims=True)))
            p_far = jnp.exp2(s_far - m)
            p_near = jnp.exp2(s_near - m)
            p_new = jnp.exp2(s_new - m)
            denom = (jnp.sum(p_far, axis=-1, keepdims=True) + jnp.sum(p_near, axis=-1, keepdims=True)
                     + jnp.sum(p_new, axis=-1, keepdims=True))
            acc = jnp.dot(p_far.astype(BF16), vb[:far_n], preferred_element_type=F32)
            acc = acc + jnp.dot(p_near.astype(BF16), vb[far_n:], preferred_element_type=F32)
            acc = acc + jnp.dot(p_new.astype(BF16), nvb, preferred_element_type=F32)
            outs.append(acc / denom)
        o = outs[0] - lam * outs[1]
        o_ref[0, :, lanes] = _sub_norm(o, g_ref[...], lam_init).astype(BF16)


def _attn_cache_call(q, cache_k, cache_v, k_new, v_new, b_near, b_new, dl, g, layer, lam_init):
    bsz, s, width = q.shape
    heads = width // V_DIM
    cache_spec = pl.BlockSpec((1, 1) + cache_k.shape[2:], lambda b: (layer, b, 0, 0))
    tile_spec = pl.BlockSpec((1, s, width), lambda b: (b, 0, 0))
    return pl.pallas_call(
        functools.partial(_attn_cache_kernel, lam_init, heads),
        out_shape=jax.ShapeDtypeStruct((bsz, s, width), BF16),
        grid=(bsz,),
        in_specs=[tile_spec, cache_spec, cache_spec, tile_spec, tile_spec,
                  _const_spec(b_near.shape), _const_spec(b_new.shape), _const_spec(dl.shape), _const_spec(g.shape)],
        out_specs=tile_spec,
        compiler_params=_params(("arbitrary",)),
        name="attn_sample",
    )(q, cache_k, cache_v, k_new, v_new, b_near, b_new, dl, g)


def _block_diag(w):
    groups, c, _ = w.shape
    eye = jnp.eye(groups, dtype=w.dtype)
    return (eye[:, None, :, None] * w[:, :, None, :]).reshape(groups * c, groups * c)


def _pad_front(state, rows):
    return jnp.pad(state, ((0, 0), (rows - state.shape[1], 0), (0, 0)))


def _layer(x, lw, layer, depth, lam_init, alpha, pool_state, conv_state, k_all, v_all, pos0, attend):
    bsz, s, d = x.shape
    x1 = _ffn_call(x.reshape(bsz * s, d), lw["wi1"], lw["wo1"], lw["g0"], lw["b0"], alpha)
    rows = min(TOKEN_TILE, s)
    q, kb, vb, k_all, v_all, pc, ptail, ctail = _inproj_call(
        x1.reshape(bsz, s, d), lw["w_in"], pool_state, conv_state, lw["pool_w"], lw["pool_scale"], lw["conv_w"],
        k_all, v_all, layer, depth, pos0, rows)
    attn = attend(q, kb, vb)
    x3 = _mixffn_call(x1, pc.reshape(bsz * s, -1), attn.reshape(bsz * s, -1), lw["w_out"], lw["g1"], lw["b1"],
                      lw["wi2"], lw["wo2"], lw["g2"], lw["b2"], alpha)
    return x3.reshape(bsz, s, d), k_all, v_all, ptail, ctail


def kernel(x_prompt, x_sample, cache_k, cache_v, state_pool, state_conv, ln_g, ln_b, w_ffn_in, w_ffn_out,
           w_in, w_out, pool_w, pool_scale, conv_w, diff_lambda, subln_g, rel_bias):
    depth = w_in.shape[0]
    bp, sp, d = x_prompt.shape
    bs, ss, _ = x_sample.shape
    past = cache_k.shape[2]
    heads = cache_k.shape[3]
    alpha = (2 * depth) ** 0.25
    t = min(ATTN_TILE, sp)

    d0 = _bias_call(rel_bias, 0, 0, t, t)
    d1 = _bias_call(rel_bias, t, t - SUB, SUB, SUB)
    b_near = _bias_call(rel_bias, past, past - NEAR_WINDOW, ss, NEAR_WINDOW)
    b_new = _bias_call(rel_bias, past, past, ss, ss)

    ck = cache_k.reshape(depth, bs, past * heads, V_DIM)
    cv = cache_v.reshape(depth, bs, past * heads, V_DIM)
    zeros_pool = jnp.zeros((bp, POOL_HIST, state_pool.shape[-1]), F32)
    zeros_conv = jnp.zeros((bp, CONV_HIST, state_conv.shape[-1]), F32)

    hp, hs = x_prompt, x_sample
    kp = vp = ks = vs = None
    tails = {"pp": [], "cp": [], "ps": [], "cs": []}
    for l in range(depth):
        lam_init = 0.8 - 0.6 * math.exp(-0.3 * l)
        lw = {
            "wi1": w_ffn_in[l, 0].astype(BF16), "wo1": w_ffn_out[l, 0].astype(BF16),
            "wi2": w_ffn_in[l, 1].astype(BF16), "wo2": w_ffn_out[l, 1].astype(BF16),
            "w_in": w_in[l].astype(BF16), "w_out": w_out[l].astype(BF16),
            "pool_w": _block_diag(pool_w[l]).astype(BF16),
            "pool_scale": pool_scale[l][None, :], "conv_w": conv_w[l],
            "g0": ln_g[l, 0][None, :], "b0": ln_b[l, 0][None, :],
            "g1": ln_g[l, 1][None, :], "b1": ln_b[l, 1][None, :],
            "g2": ln_g[l, 2][None, :], "b2": ln_b[l, 2][None, :],
        }
        dl = diff_lambda[l]
        g = subln_g[l][None, :]

        def attend_prompt(q, kb, vb, lam_init=lam_init, dl=dl, g=g):
            return _attn_call(q, kb, vb, d0, d1, dl, g, lam_init)

        def attend_sample(q, kb, vb, l=l, lam_init=lam_init, dl=dl, g=g):
            return _attn_cache_call(q, ck, cv, kb, vb, b_near, b_new, dl, g, l, lam_init)

        hp, kp, vp, pt, ct = _layer(hp, lw, l, depth, lam_init, alpha, zeros_pool, zeros_conv, kp, vp, 0,
                                    attend_prompt)
        tails["pp"].append(pt)
        tails["cp"].append(ct)
        hs, ks, vs, pt, ct = _layer(hs, lw, l, depth, lam_init, alpha, _pad_front(state_pool[l], POOL_HIST),
                                    _pad_front(state_conv[l], CONV_HIST), ks, vs, past, attend_sample)
        tails["ps"].append(pt)
        tails["cs"].append(ct)

    def states(parts, keep):
        return jnp.stack(parts)[:, :, -keep:, :]

    return (hp, hs,
            kp.reshape(depth, bp, sp, heads, V_DIM), vp.reshape(depth, bp, sp, heads, V_DIM),
            states(tails["pp"], POOL_STATE), states(tails["cp"], CONV_K - 1),
            ks.reshape(depth, bs, ss, heads, V_DIM), vs.reshape(depth, bs, ss, heads, V_DIM),
            states(tails["ps"], POOL_STATE), states(tails["cs"], CONV_K - 1))
```

```python
import functools
import math

import jax
import jax.numpy as jnp
from jax import lax
from jax.experimental import pallas as pl
from jax.experimental.pallas import tpu as pltpu

F32 = jnp.float32
BF16 = jnp.bfloat16

CHUNK = 64
POOL_WINDOWS = (2, 4, 8, 16)
POOL_STATE = max(POOL_WINDOWS) - 1
CONV_K = 3
QK_DIM = 64
V_DIM = 2 * QK_DIM
NUM_BUCKETS = 32
MAX_DISTANCE = 128
LN_EPS = 1e-5
RMS_EPS = 1e-5
NEG_INF = -1e30
LOG2E = math.log2(math.e)

LANES = 128
SUBLANES = 8
MXU_DIM = 256
VMEM_LIMIT = 56 * 1024 * 1024

TOKEN_TILE = 512
ATTN_TILE = 512
FF_CHUNK = MXU_DIM
POOL_HIST = 16
CONV_HIST = 8
NEAR_WINDOW = 256


def _const_spec(shape):
    nd = len(shape)
    return pl.BlockSpec(shape, lambda *_: (0,) * nd, pipeline_mode=pl.Buffered(1))


def _params(sem):
    return pltpu.CompilerParams(dimension_semantics=sem, vmem_limit_bytes=VMEM_LIMIT)


def _layer_norm(z, g, b):
    mu = jnp.mean(z, axis=-1, keepdims=True)
    zc = z - mu
    var = jnp.mean(zc * zc, axis=-1, keepdims=True)
    return zc * lax.rsqrt(var + LN_EPS) * g + b


def _swiglu(x, wi_ref, wo_ref, h_ref):
    d_ff = wo_ref.shape[0]
    xb = x.astype(BF16)
    for c in range(d_ff // FF_CHUNK):
        lo = c * FF_CHUNK
        gate = jnp.dot(xb, wi_ref[:, lo:lo + FF_CHUNK], preferred_element_type=F32)
        up = jnp.dot(xb, wi_ref[:, d_ff + lo:d_ff + lo + FF_CHUNK], preferred_element_type=F32)
        h_ref[:, lo:lo + FF_CHUNK] = (gate * jax.nn.sigmoid(gate) * up).astype(BF16)
    return jnp.dot(h_ref[...], wo_ref[...], preferred_element_type=F32)


def _ffn_kernel(alpha, x_ref, wi_ref, wo_ref, g_ref, b_ref, o_ref, h_ref):
    x = x_ref[...]
    y = _swiglu(x, wi_ref, wo_ref, h_ref)
    o_ref[...] = _layer_norm(alpha * x + 0.5 * y, g_ref[...], b_ref[...])


def _mixffn_kernel(alpha, x_ref, pc_ref, at_ref, wm_ref, g1_ref, b1_ref,
                   wi_ref, wo_ref, g2_ref, b2_ref, o_ref, h_ref):
    n_pc = pc_ref.shape[1]
    mix = jnp.dot(pc_ref[...], wm_ref[0:n_pc, :], preferred_element_type=F32)
    mix = mix + jnp.dot(at_ref[...], wm_ref[n_pc:, :], preferred_element_type=F32)
    x = _layer_norm(alpha * x_ref[...] + mix, g1_ref[...], b1_ref[...])
    y = _swiglu(x, wi_ref, wo_ref, h_ref)
    o_ref[...] = _layer_norm(alpha * x + 0.5 * y, g2_ref[...], b2_ref[...])


def _ffn_call(x, wi, wo, g, b, alpha):
    m, d = x.shape
    tm = min(TOKEN_TILE, m)
    d_ff = wo.shape[0]
    row = pl.BlockSpec((tm, d), lambda i: (i, 0))
    return pl.pallas_call(
        functools.partial(_ffn_kernel, alpha),
        out_shape=jax.ShapeDtypeStruct((m, d), F32),
        grid=(m // tm,),
        in_specs=[row, _const_spec(wi.shape), _const_spec(wo.shape),
                  _const_spec(g.shape), _const_spec(b.shape)],
        out_specs=row,
        scratch_shapes=[pltpu.VMEM((tm, d_ff), BF16)],
        compiler_params=_params(("arbitrary",)),
        name="ffn",
    )(x, wi, wo, g, b)


def _mixffn_call(x, pc, at, wm, g1, b1, wi, wo, g2, b2, alpha):
    m, d = x.shape
    tm = min(TOKEN_TILE, m)
    d_ff = wo.shape[0]
    row = pl.BlockSpec((tm, d), lambda i: (i, 0))
    half = pl.BlockSpec((tm, pc.shape[1]), lambda i: (i, 0))
    half2 = pl.BlockSpec((tm, at.shape[1]), lambda i: (i, 0))
    return pl.pallas_call(
        functools.partial(_mixffn_kernel, alpha),
        out_shape=jax.ShapeDtypeStruct((m, d), F32),
        grid=(m // tm,),
        in_specs=[row, half, half2, _const_spec(wm.shape), _const_spec(g1.shape), _const_spec(b1.shape),
                  _const_spec(wi.shape), _const_spec(wo.shape), _const_spec(g2.shape), _const_spec(b2.shape)],
        out_specs=row,
        scratch_shapes=[pltpu.VMEM((tm, d_ff), BF16)],
        compiler_params=_params(("arbitrary",)),
        name="mixffn",
    )(x, pc, at, wm, g1, b1, wi, wo, g2, b2)


def _inproj_kernel(pos0, n_prev, x_ref, w_ref, pst_ref, cst_ref, pw_ref, ps_ref, cw_ref, *refs):
    prev_refs, refs = refs[:2 if n_prev else 0], refs[2 if n_prev else 0:]
    q_ref, kb_ref, vb_ref, k_ref, v_ref, pc_ref, pt_ref, ct_ref, pext_ref, cext_ref = refs
    nb, rows, d = x_ref.shape
    pool_w = pst_ref.shape[2]
    conv_w = cst_ref.shape[2]
    attn_w = q_ref.shape[2]
    heads = attn_w // V_DIM
    i = pl.program_id(1)

    @pl.when(i == 0)
    def _():
        pext_ref[:, 0:POOL_HIST, :] = pst_ref[...]
        cext_ref[:, 0:CONV_HIST, :] = cst_ref[...]

    @pl.when(i > 0)
    def _():
        pext_ref[:, 0:POOL_HIST, :] = pext_ref[:, rows:rows + POOL_HIST, :]
        cext_ref[:, 0:CONV_HIST, :] = cext_ref[:, rows:rows + CONV_HIST, :]

    xb = x_ref[...].reshape(nb * rows, d).astype(BF16)

    def proj(lo, n):
        return jnp.dot(xb, w_ref[:, lo:lo + n], preferred_element_type=F32)

    off = 0
    u = proj(off, pool_w); off += pool_w
    b_gate = proj(off, conv_w); off += conv_w
    c_gate = proj(off, conv_w); off += conv_w
    h_conv = proj(off, conv_w); off += conv_w
    pext_ref[:, POOL_HIST:, :] = u.reshape(nb, rows, pool_w)
    cext_ref[:, CONV_HIST:, :] = (c_gate * h_conv).reshape(nb, rows, conv_w)

    group = pool_w // len(POOL_WINDOWS)
    lane = lax.broadcasted_iota(jnp.int32, (1, pool_w), 1)
    pos = pos0 + i * rows + lax.broadcasted_iota(jnp.int32, (rows, 1), 0)
    cnt = [jnp.minimum(w, pos + 1).astype(F32) for w in POOL_WINDOWS]
    count = jnp.where(lane < group, cnt[0],
                      jnp.where(lane < 2 * group, cnt[1],
                                jnp.where(lane < 3 * group, cnt[2], cnt[3])))
    for n in range(nb):
        e = pext_ref[n]
        s2 = e + pltpu.roll(e, 1, 0)
        s4 = s2 + pltpu.roll(s2, 2, 0)
        s8 = s4 + pltpu.roll(s4, 4, 0)
        s16 = s8 + pltpu.roll(s8, 8, 0)
        win = jnp.where(lane < group, s2,
                        jnp.where(lane < 2 * group, s4,
                                  jnp.where(lane < 3 * group, s8, s16)))[POOL_HIST:]
        diff = (win / count - e[POOL_HIST:]).astype(BF16)
        pooled = jnp.dot(diff, pw_ref[...], preferred_element_type=F32) * ps_ref[...]
        pc_ref[n, :, 0:pool_w] = pooled.astype(BF16)

        z = cext_ref[n]
        conv = cw_ref[2:3, :] * z + cw_ref[1:2, :] * pltpu.roll(z, 1, 0) + cw_ref[0:1, :] * pltpu.roll(z, 2, 0)
        gated = b_gate[n * rows:(n + 1) * rows] * conv[CONV_HIST:]
        pc_ref[n, :, pool_w:pool_w + conv_w] = gated.astype(BF16)

    pt_ref[...] = pext_ref[:, rows:rows + POOL_HIST, :]
    ct_ref[...] = cext_ref[:, rows:rows + CONV_HIST, :]

    q = proj(off, attn_w); off += attn_w
    q_ref[...] = (q * (QK_DIM ** -0.5 * LOG2E)).astype(BF16).reshape(nb, rows, attn_w)
    for idx, (dense_ref, out_ref) in enumerate(((kb_ref, k_ref), (vb_ref, v_ref))):
        kv = proj(off, attn_w); off += attn_w
        dense_ref[...] = kv.astype(BF16).reshape(nb, rows, attn_w)
        if n_prev:
            out_ref[0:n_prev] = prev_refs[idx][...]
        for n in range(nb):
            for hd in range(heads):
                out_ref[n_prev, n, pl.ds(hd, rows, stride=heads), :] = kv[n * rows:(n + 1) * rows,
                                                                          hd * V_DIM:(hd + 1) * V_DIM]


def _inproj_call(x, w_in, pool_state, conv_state, pool_w_bd, pool_scale, conv_w, k_prev, v_prev, pos0, rows):
    bsz, s, d = x.shape
    nb = bsz if rows == s and bsz * s <= TOKEN_TILE else 1
    attn_w = (w_in.shape[1] - pool_state.shape[2] - 3 * conv_state.shape[2]) // 3
    heads = attn_w // V_DIM
    pool_w = pool_state.shape[2]
    conv_dim = conv_state.shape[2]
    n_prev = 0 if k_prev is None else k_prev.shape[0]

    def tile(width):
        return pl.BlockSpec((nb, rows, width), lambda b, i: (b, i, 0))

    def layers_spec(n):
        return pl.BlockSpec((n, nb, rows * heads, V_DIM), lambda b, i: (0, b, i, 0))

    in_specs = [tile(d), _const_spec(w_in.shape),
                pl.BlockSpec((nb, POOL_HIST, pool_w), lambda b, i: (b, 0, 0)),
                pl.BlockSpec((nb, CONV_HIST, conv_dim), lambda b, i: (b, 0, 0)),
                _const_spec(pool_w_bd.shape), _const_spec(pool_scale.shape), _const_spec(conv_w.shape)]
    args = [x, w_in, pool_state, conv_state, pool_w_bd, pool_scale, conv_w]
    if n_prev:
        in_specs += [layers_spec(n_prev), layers_spec(n_prev)]
        args += [k_prev, v_prev]
    kv_shape = jax.ShapeDtypeStruct((n_prev + 1, bsz, s * heads, V_DIM), F32)
    dense = jax.ShapeDtypeStruct((bsz, s, attn_w), BF16)
    out_shape = (dense, dense, dense, kv_shape, kv_shape,
                 jax.ShapeDtypeStruct((bsz, s, pool_w + conv_dim), BF16),
                 jax.ShapeDtypeStruct((bsz, POOL_HIST, pool_w), F32),
                 jax.ShapeDtypeStruct((bsz, CONV_HIST, conv_dim), F32))
    out_specs = (tile(attn_w), tile(attn_w), tile(attn_w), layers_spec(n_prev + 1), layers_spec(n_prev + 1),
                 tile(pool_w + conv_dim),
                 pl.BlockSpec((nb, POOL_HIST, pool_w), lambda b, i: (b, 0, 0)),
                 pl.BlockSpec((nb, CONV_HIST, conv_dim), lambda b, i: (b, 0, 0)))
    return pl.pallas_call(
        functools.partial(_inproj_kernel, pos0, n_prev),
        out_shape=out_shape,
        grid=(bsz // nb, s // rows),
        in_specs=in_specs,
        out_specs=out_specs,
        scratch_shapes=[pltpu.VMEM((nb, POOL_HIST + rows, pool_w), F32),
                        pltpu.VMEM((nb, CONV_HIST + rows, conv_dim), F32)],
        compiler_params=_params(("arbitrary", "arbitrary")),
        name="inproj",
    )(*args)


def _bias_kernel(q0, k0, ref_bucket, rb_ref, o_ref):
    h = pl.program_id(0)
    _, nr, nc = o_ref.shape
    qp = q0 + lax.broadcasted_iota(jnp.int32, (nr, nc), 0)
    kp = k0 + lax.broadcasted_iota(jnp.int32, (nr, nc), 1)
    rel = kp - qp
    n = jnp.abs(rel)
    nb = NUM_BUCKETS // 2
    max_exact = nb // 2
    steps = nb - max_exact
    large = jnp.full((nr, nc), max_exact, jnp.int32)
    for j in range(1, steps):
        thr = math.ceil(max_exact * (MAX_DISTANCE / max_exact) ** (j / steps) - 1e-9)
        large = large + (n >= thr).astype(jnp.int32)
    bucket = jnp.where(rel > 0, nb, 0) + jnp.where(n < max_exact, n, large)
    val = jnp.zeros((nr, nc), F32)
    for bkt in range(NUM_BUCKETS):
        val = jnp.where(bucket == bkt, rb_ref[bkt, h], val)
    val = (val - rb_ref[ref_bucket, h]) * LOG2E
    visible = (kp // CHUNK) <= (qp // CHUNK)
    o_ref[0] = jnp.where(visible, val, NEG_INF)


def _bias_call(rel_bias, q0, k0, nr, nc):
    heads = rel_bias.shape[1]
    ref_bucket = NUM_BUCKETS // 2 - 1
    return pl.pallas_call(
        functools.partial(_bias_kernel, q0, k0, ref_bucket),
        out_shape=jax.ShapeDtypeStruct((heads, nr, nc), F32),
        grid=(heads,),
        in_specs=[pl.BlockSpec(memory_space=pltpu.SMEM)],
        out_specs=pl.BlockSpec((1, nr, nc), lambda h: (h, 0, 0)),
        compiler_params=_params(("arbitrary",)),
        name="relbias",
    )(rel_bias)


def _lambda_value(dl_ref, lam_init):
    dl = dl_ref[...]
    a = jnp.sum(dl[0:1] * dl[1:2], axis=-1, keepdims=True)
    b = jnp.sum(dl[2:3] * dl[3:4], axis=-1, keepdims=True)
    return jnp.exp(a) - jnp.exp(b) + lam_init


def _split_components(q):
    lane = lax.broadcasted_iota(jnp.int32, q.shape, 1)
    zero = jnp.zeros_like(q)
    return jnp.where(lane < QK_DIM, q, zero), jnp.where(lane >= QK_DIM, q, zero)


def _scores(qm, kt):
    return lax.dot_general(qm, kt, (((1,), (1,)), ((), ())), preferred_element_type=F32)


def _sub_norm(o, g, lam_init):
    o = o * lax.rsqrt(jnp.mean(o * o, axis=-1, keepdims=True) + RMS_EPS) * g
    return o * (1.0 - lam_init)


def _lane_blocks(s):
    return [s[:, n * LANES:(n + 1) * LANES] for n in range(s.shape[1] // LANES)]


def _tree(op, xs):
    while len(xs) > 1:
        xs = [op(xs[n], xs[n + 1]) if n + 1 < len(xs) else xs[n] for n in range(0, len(xs), 2)]
    return xs[0]


def _attn_kernel(lam_init, q_ref, k_ref, v_ref, d0_ref, d1_ref, dl_ref, g_ref, o_ref, m_ref, l_ref, acc_ref):
    i = pl.program_id(2)
    t = q_ref.shape[1]
    qms = _split_components(q_ref[0])
    m_ref[...] = jnp.full(m_ref.shape, NEG_INF, F32)
    l_ref[...] = jnp.zeros(l_ref.shape, F32)
    acc_ref[...] = jnp.zeros(acc_ref.shape, F32)

    def step(j, bias):
        off = pl.multiple_of(j * t, t)
        kt = k_ref[0, pl.ds(off, t), :]
        vt = v_ref[0, pl.ds(off, t), :]
        scores = [_scores(qm, kt) for qm in qms]
        if bias is not None:
            scores = [s + bias for s in scores]
        for c, s in enumerate(scores):
            blocks = _lane_blocks(s)
            m_prev = m_ref[c]
            m_cur = jnp.maximum(m_prev, jnp.max(_tree(jnp.maximum, blocks), axis=-1, keepdims=True))
            scale = jnp.exp2(m_prev - m_cur)
            ps = [jnp.exp2(blk - m_cur) for blk in blocks]
            l_ref[c] = scale * l_ref[c] + _tree(jnp.add, ps)
            p = jnp.concatenate([x.astype(BF16) for x in ps], axis=1)
            acc_ref[c] = scale * acc_ref[c] + jnp.dot(p, vt, preferred_element_type=F32)
            m_ref[c] = m_cur

    def far(j, carry):
        step(j, None)
        return carry

    lax.fori_loop(0, jnp.maximum(i - 1, 0), far, 0)

    @pl.when(i > 0)
    def _():
        step(i - 1, d1_ref[0])

    step(i, d0_ref[0])

    lam = _lambda_value(dl_ref, lam_init)
    outs = [acc_ref[c] / jnp.sum(l_ref[c], axis=-1, keepdims=True) for c in range(2)]
    o_ref[0] = _sub_norm(outs[0] - lam * outs[1], g_ref[...], lam_init).astype(BF16)


def _attn_call(q, k, v, d0, d1, dl, g, lam_init):
    bsz, s, width = q.shape
    heads = width // V_DIM
    t = min(ATTN_TILE, s)
    kv_spec = pl.BlockSpec((1, s, V_DIM), lambda b, h, i: (b, 0, h))
    tile_spec = pl.BlockSpec((1, t, V_DIM), lambda b, h, i: (b, i, h))
    return pl.pallas_call(
        functools.partial(_attn_kernel, lam_init),
        out_shape=jax.ShapeDtypeStruct((bsz, s, width), BF16),
        grid=(bsz, heads, s // t),
        in_specs=[tile_spec, kv_spec, kv_spec,
                  pl.BlockSpec((1, t, t), lambda b, h, i: (h, 0, 0)),
                  pl.BlockSpec((1, t, t), lambda b, h, i: (h, 0, 0)),
                  pl.BlockSpec(dl.shape, lambda b, h, i: (0, 0)),
                  pl.BlockSpec(g.shape, lambda b, h, i: (0, 0))],
        out_specs=tile_spec,
        scratch_shapes=[pltpu.VMEM((2, t, LANES), F32), pltpu.VMEM((2, t, LANES), F32),
                        pltpu.VMEM((2, t, V_DIM), F32)],
        compiler_params=_params(("arbitrary", "arbitrary", "arbitrary")),
        name="attn_prompt",
    )(q, k, v, d0, d1, dl, g)


def _attn_cache_kernel(lam_init, heads, q_ref, ck_ref, cv_ref, nk_ref, nv_ref, bn_ref, bw_ref, dl_ref, g_ref,
                       o_ref):
    past = ck_ref.shape[2] // heads
    far_n = past - NEAR_WINDOW
    lam = _lambda_value(dl_ref, lam_init)
    for hd in range(heads):
        lanes = slice(hd * V_DIM, (hd + 1) * V_DIM)
        kb = ck_ref[0, 0, pl.ds(hd, past, stride=heads), :].astype(BF16)
        vb = cv_ref[0, 0, pl.ds(hd, past, stride=heads), :].astype(BF16)
        nkb = nk_ref[0, :, lanes]
        nvb = nv_ref[0, :, lanes]
        outs = []
        for qm in _split_components(q_ref[0, :, lanes]):
            s_far = _scores(qm, kb[:far_n])
            s_near = _scores(qm, kb[far_n:]) + bn_ref[hd]
            s_new = _scores(qm, nkb) + bw_ref[hd]
            m = jnp.maximum(jnp.max(s_far, axis=-1, keepdims=True),
                            jnp.maximum(jnp.max(s_near, axis=-1, keepdims=True),
                                        jnp.max(s_new, axis=-1, keepdims=True)))
            p_far = jnp.exp2(s_far - m)
            p_near = jnp.exp2(s_near - m)
            p_new = jnp.exp2(s_new - m)
            denom = (jnp.sum(p_far, axis=-1, keepdims=True) + jnp.sum(p_near, axis=-1, keepdims=True)
                     + jnp.sum(p_new, axis=-1, keepdims=True))
            acc = jnp.dot(p_far.astype(BF16), vb[:far_n], preferred_element_type=F32)
            acc = acc + jnp.dot(p_near.astype(BF16), vb[far_n:], preferred_element_type=F32)
            acc = acc + jnp.dot(p_new.astype(BF16), nvb, preferred_element_type=F32)
            outs.append(acc / denom)
        o = outs[0] - lam * outs[1]
        o_ref[0, :, lanes] = _sub_norm(o, g_ref[...], lam_init).astype(BF16)


def _attn_cache_call(q, cache_k, cache_v, k_new, v_new, b_near, b_new, dl, g, layer, lam_init):
    bsz, s, width = q.shape
    heads = width // V_DIM
    cache_spec = pl.BlockSpec((1, 1) + cache_k.shape[2:], lambda b: (layer, b, 0, 0))
    tile_spec = pl.BlockSpec((1, s, width), lambda b: (b, 0, 0))
    return pl.pallas_call(
        functools.partial(_attn_cache_kernel, lam_init, heads),
        out_shape=jax.ShapeDtypeStruct((bsz, s, width), BF16),
        grid=(bsz,),
        in_specs=[tile_spec, cache_spec, cache_spec, tile_spec, tile_spec,
                  _const_spec(b_near.shape), _const_spec(b_new.shape), _const_spec(dl.shape), _const_spec(g.shape)],
        out_specs=tile_spec,
        compiler_params=_params(("arbitrary",)),
        name="attn_sample",
    )(q, cache_k, cache_v, k_new, v_new, b_near, b_new, dl, g)


def _block_diag(w):
    groups, c, _ = w.shape
    eye = jnp.eye(groups, dtype=w.dtype)
    return (eye[:, None, :, None] * w[:, :, None, :]).reshape(groups * c, groups * c)


def _pad_front(state, rows):
    return jnp.pad(state, ((0, 0), (rows - state.shape[1], 0), (0, 0)))


def _layer(x, lw, alpha, pool_state, conv_state, k_all, v_all, pos0, attend):
    bsz, s, d = x.shape
    x1 = _ffn_call(x.reshape(bsz * s, d), lw["wi1"], lw["wo1"], lw["g0"], lw["b0"], alpha)
    rows = min(TOKEN_TILE, s)
    q, kb, vb, k_all, v_all, pc, ptail, ctail = _inproj_call(
        x1.reshape(bsz, s, d), lw["w_in"], pool_state, conv_state, lw["pool_w"], lw["pool_scale"], lw["conv_w"],
        k_all, v_all, pos0, rows)
    attn = attend(q, kb, vb)
    x3 = _mixffn_call(x1, pc.reshape(bsz * s, -1), attn.reshape(bsz * s, -1), lw["w_out"], lw["g1"], lw["b1"],
                      lw["wi2"], lw["wo2"], lw["g2"], lw["b2"], alpha)
    return x3.reshape(bsz, s, d), k_all, v_all, ptail, ctail


def kernel(x_prompt, x_sample, cache_k, cache_v, state_pool, state_conv, ln_g, ln_b, w_ffn_in, w_ffn_out,
           w_in, w_out, pool_w, pool_scale, conv_w, diff_lambda, subln_g, rel_bias):
    depth = w_in.shape[0]
    bp, sp, d = x_prompt.shape
    bs, ss, _ = x_sample.shape
    past = cache_k.shape[2]
    heads = cache_k.shape[3]
    alpha = (2 * depth) ** 0.25
    t = min(ATTN_TILE, sp)

    d0 = _bias_call(rel_bias, 0, 0, t, t)
    d1 = _bias_call(rel_bias, t, 0, t, t)
    b_near = _bias_call(rel_bias, past, past - NEAR_WINDOW, ss, NEAR_WINDOW)
    b_new = _bias_call(rel_bias, past, past, ss, ss)

    ck = cache_k.reshape(depth, bs, past * heads, V_DIM)
    cv = cache_v.reshape(depth, bs, past * heads, V_DIM)
    zeros_pool = jnp.zeros((bp, POOL_HIST, state_pool.shape[-1]), F32)
    zeros_conv = jnp.zeros((bp, CONV_HIST, state_conv.shape[-1]), F32)

    hp, hs = x_prompt, x_sample
    kp = vp = ks = vs = None
    tails = {"pp": [], "cp": [], "ps": [], "cs": []}
    for l in range(depth):
        lam_init = 0.8 - 0.6 * math.exp(-0.3 * l)
        lw = {
            "wi1": w_ffn_in[l, 0].astype(BF16), "wo1": w_ffn_out[l, 0].astype(BF16),
            "wi2": w_ffn_in[l, 1].astype(BF16), "wo2": w_ffn_out[l, 1].astype(BF16),
            "w_in": w_in[l].astype(BF16), "w_out": w_out[l].astype(BF16),
            "pool_w": _block_diag(pool_w[l]).astype(BF16),
            "pool_scale": pool_scale[l][None, :], "conv_w": conv_w[l],
            "g0": ln_g[l, 0][None, :], "b0": ln_b[l, 0][None, :],
            "g1": ln_g[l, 1][None, :], "b1": ln_b[l, 1][None, :],
            "g2": ln_g[l, 2][None, :], "b2": ln_b[l, 2][None, :],
        }
        dl = diff_lambda[l]
        g = subln_g[l][None, :]

        def attend_prompt(q, kb, vb, lam_init=lam_init, dl=dl, g=g):
            return _attn_call(q, kb, vb, d0, d1, dl, g, lam_init)

        def attend_sample(q, kb, vb, l=l, lam_init=lam_init, dl=dl, g=g):
            return _attn_cache_call(q, ck, cv, kb, vb, b_near, b_new, dl, g, l, lam_init)

        hp, kp, vp, pt, ct = _layer(hp, lw, alpha, zeros_pool, zeros_conv, kp, vp, 0, attend_prompt)
        tails["pp"].append(pt)
        tails["cp"].append(ct)
        hs, ks, vs, pt, ct = _layer(hs, lw, alpha, _pad_front(state_pool[l], POOL_HIST),
                                    _pad_front(state_conv[l], CONV_HIST), ks, vs, past, attend_sample)
        tails["ps"].append(pt)
        tails["cs"].append(ct)

    def states(parts, keep):
        return jnp.stack(parts)[:, :, -keep:, :]

    return (hp, hs,
            kp.reshape(depth, bp, sp, heads, V_DIM), vp.reshape(depth, bp, sp, heads, V_DIM),
            states(tails["pp"], POOL_STATE), states(tails["cp"], CONV_K - 1),
            ks.reshape(depth, bs, ss, heads, V_DIM), vs.reshape(depth, bs, ss, heads, V_DIM),
            states(tails["ps"], POOL_STATE), states(tails["cs"], CONV_K - 1))
```

```python
import functools
import math

import jax
import jax.numpy as jnp
from jax import lax
from jax.experimental import pallas as pl
from jax.experimental.pallas import tpu as pltpu

F32 = jnp.float32
BF16 = jnp.bfloat16

CHUNK = 64
POOL_WINDOWS = (2, 4, 8, 16)
POOL_STATE = max(POOL_WINDOWS) - 1
CONV_K = 3
QK_DIM = 64
V_DIM = 2 * QK_DIM
NUM_BUCKETS = 32
MAX_DISTANCE = 128
LN_EPS = 1e-5
RMS_EPS = 1e-5
NEG_INF = -1e30
LOG2E = math.log2(math.e)

LANES = 128
SUBLANES = 8
MXU_DIM = 256
VMEM_LIMIT = 56 * 1024 * 1024

TOKEN_TILE = 512
ATTN_TILE = 512
FF_CHUNK = MXU_DIM
POOL_HIST = 16
CONV_HIST = 8
NEAR_WINDOW = 256


def _const_spec(shape):
    nd = len(shape)
    return pl.BlockSpec(shape, lambda *_: (0,) * nd, pipeline_mode=pl.Buffered(1))


def _params(sem):
    return pltpu.CompilerParams(dimension_semantics=sem, vmem_limit_bytes=VMEM_LIMIT)


def _layer_norm(z, g, b):
    mu = jnp.mean(z, axis=-1, keepdims=True)
    zc = z - mu
    var = jnp.mean(zc * zc, axis=-1, keepdims=True)
    return zc * lax.rsqrt(var + LN_EPS) * g + b


def _swiglu(x, wi_ref, wo_ref, h_ref):
    d_ff = wo_ref.shape[0]
    xb = x.astype(BF16)
    for c in range(d_ff // FF_CHUNK):
        lo = c * FF_CHUNK
        gate = jnp.dot(xb, wi_ref[:, lo:lo + FF_CHUNK], preferred_element_type=F32)
        up = jnp.dot(xb, wi_ref[:, d_ff + lo:d_ff + lo + FF_CHUNK], preferred_element_type=F32)
        h_ref[:, lo:lo + FF_CHUNK] = (gate * jax.nn.sigmoid(gate) * up).astype(BF16)
    return jnp.dot(h_ref[...], wo_ref[...], preferred_element_type=F32)


def _ffn_kernel(alpha, x_ref, wi_ref, wo_ref, g_ref, b_ref, o_ref, h_ref):
    x = x_ref[...]
    y = _swiglu(x, wi_ref, wo_ref, h_ref)
    o_ref[...] = _layer_norm(alpha * x + 0.5 * y, g_ref[...], b_ref[...])


def _mixffn_kernel(alpha, x_ref, pc_ref, at_ref, wm_ref, g1_ref, b1_ref,
                   wi_ref, wo_ref, g2_ref, b2_ref, o_ref, h_ref):
    n_pc = pc_ref.shape[1]
    mix = jnp.dot(pc_ref[...], wm_ref[0:n_pc, :], preferred_element_type=F32)
    mix = mix + jnp.dot(at_ref[...], wm_ref[n_pc:, :], preferred_element_type=F32)
    x = _layer_norm(alpha * x_ref[...] + mix, g1_ref[...], b1_ref[...])
    y = _swiglu(x, wi_ref, wo_ref, h_ref)
    o_ref[...] = _layer_norm(alpha * x + 0.5 * y, g2_ref[...], b2_ref[...])


def _ffn_call(x, wi, wo, g, b, alpha):
    m, d = x.shape
    tm = min(TOKEN_TILE, m)
    d_ff = wo.shape[0]
    row = pl.BlockSpec((tm, d), lambda i: (i, 0))
    return pl.pallas_call(
        functools.partial(_ffn_kernel, alpha),
        out_shape=jax.ShapeDtypeStruct((m, d), F32),
        grid=(m // tm,),
        in_specs=[row, _const_spec(wi.shape), _const_spec(wo.shape),
                  _const_spec(g.shape), _const_spec(b.shape)],
        out_specs=row,
        scratch_shapes=[pltpu.VMEM((tm, d_ff), BF16)],
        compiler_params=_params(("arbitrary",)),
        name="ffn",
    )(x, wi, wo, g, b)


def _mixffn_call(x, pc, at, wm, g1, b1, wi, wo, g2, b2, alpha):
    m, d = x.shape
    tm = min(TOKEN_TILE, m)
    d_ff = wo.shape[0]
    row = pl.BlockSpec((tm, d), lambda i: (i, 0))
    half = pl.BlockSpec((tm, pc.shape[1]), lambda i: (i, 0))
    half2 = pl.BlockSpec((tm, at.shape[1]), lambda i: (i, 0))
    return pl.pallas_call(
        functools.partial(_mixffn_kernel, alpha),
        out_shape=jax.ShapeDtypeStruct((m, d), F32),
        grid=(m // tm,),
        in_specs=[row, half, half2, _const_spec(wm.shape), _const_spec(g1.shape), _const_spec(b1.shape),
                  _const_spec(wi.shape), _const_spec(wo.shape), _const_spec(g2.shape), _const_spec(b2.shape)],
        out_specs=row,
        scratch_shapes=[pltpu.VMEM((tm, d_ff), BF16)],
        compiler_params=_params(("arbitrary",)),
        name="mixffn",
    )(x, pc, at, wm, g1, b1, wi, wo, g2, b2)


def _inproj_kernel(pos0, n_prev, x_ref, w_ref, pst_ref, cst_ref, pw_ref, ps_ref, cw_ref, *refs):
    prev_refs, refs = refs[:2 if n_prev else 0], refs[2 if n_prev else 0:]
    q_ref, kb_ref, vb_ref, k_ref, v_ref, pc_ref, pt_ref, ct_ref, pext_ref, cext_ref = refs
    nb, rows, d = x_ref.shape
    pool_w = pst_ref.shape[2]
    conv_w = cst_ref.shape[2]
    attn_w = q_ref.shape[2]
    heads = attn_w // V_DIM
    i = pl.program_id(1)

    @pl.when(i == 0)
    def _():
        pext_ref[:, 0:POOL_HIST, :] = pst_ref[...]
        cext_ref[:, 0:CONV_HIST, :] = cst_ref[...]

    @pl.when(i > 0)
    def _():
        pext_ref[:, 0:POOL_HIST, :] = pext_ref[:, rows:rows + POOL_HIST, :]
        cext_ref[:, 0:CONV_HIST, :] = cext_ref[:, rows:rows + CONV_HIST, :]

    xb = x_ref[...].reshape(nb * rows, d).astype(BF16)

    def proj(lo, n):
        return jnp.dot(xb, w_ref[:, lo:lo + n], preferred_element_type=F32)

    off = 0
    u = proj(off, pool_w); off += pool_w
    b_gate = proj(off, conv_w); off += conv_w
    c_gate = proj(off, conv_w); off += conv_w
    h_conv = proj(off, conv_w); off += conv_w
    pext_ref[:, POOL_HIST:, :] = u.reshape(nb, rows, pool_w)
    cext_ref[:, CONV_HIST:, :] = (c_gate * h_conv).reshape(nb, rows, conv_w)

    group = pool_w // len(POOL_WINDOWS)
    lane = lax.broadcasted_iota(jnp.int32, (1, pool_w), 1)
    pos = pos0 + i * rows + lax.broadcasted_iota(jnp.int32, (rows, 1), 0)
    cnt = [jnp.minimum(w, pos + 1).astype(F32) for w in POOL_WINDOWS]
    count = jnp.where(lane < group, cnt[0],
                      jnp.where(lane < 2 * group, cnt[1],
                                jnp.where(lane < 3 * group, cnt[2], cnt[3])))
    for n in range(nb):
        e = pext_ref[n]
        s2 = e + pltpu.roll(e, 1, 0)
        s4 = s2 + pltpu.roll(s2, 2, 0)
        s8 = s4 + pltpu.roll(s4, 4, 0)
        s16 = s8 + pltpu.roll(s8, 8, 0)
        win = jnp.where(lane < group, s2,
                        jnp.where(lane < 2 * group, s4,
                                  jnp.where(lane < 3 * group, s8, s16)))[POOL_HIST:]
        diff = (win / count - e[POOL_HIST:]).astype(BF16)
        pooled = jnp.dot(diff, pw_ref[...], preferred_element_type=F32) * ps_ref[...]
        pc_ref[n, :, 0:pool_w] = pooled.astype(BF16)

        z = cext_ref[n]
        conv = cw_ref[2:3, :] * z + cw_ref[1:2, :] * pltpu.roll(z, 1, 0) + cw_ref[0:1, :] * pltpu.roll(z, 2, 0)
        gated = b_gate[n * rows:(n + 1) * rows] * conv[CONV_HIST:]
        pc_ref[n, :, pool_w:pool_w + conv_w] = gated.astype(BF16)

    pt_ref[...] = pext_ref[:, rows:rows + POOL_HIST, :]
    ct_ref[...] = cext_ref[:, rows:rows + CONV_HIST, :]

    q = proj(off, attn_w); off += attn_w
    q_ref[...] = (q * (QK_DIM ** -0.5 * LOG2E)).astype(BF16).reshape(nb, rows, attn_w)
    for idx, (dense_ref, out_ref) in enumerate(((kb_ref, k_ref), (vb_ref, v_ref))):
        kv = proj(off, attn_w); off += attn_w
        dense_ref[...] = kv.astype(BF16).reshape(nb, rows, attn_w)
        if n_prev:
            out_ref[0:n_prev] = prev_refs[idx][...]
        for n in range(nb):
            for hd in range(heads):
                out_ref[n_prev, n, pl.ds(hd, rows, stride=heads), :] = kv[n * rows:(n + 1) * rows,
                                                                          hd * V_DIM:(hd + 1) * V_DIM]


def _inproj_call(x, w_in, pool_state, conv_state, pool_w_bd, pool_scale, conv_w, k_prev, v_prev, pos0, rows):
    bsz, s, d = x.shape
    nb = bsz if rows == s and bsz * s <= TOKEN_TILE else 1
    attn_w = (w_in.shape[1] - pool_state.shape[2] - 3 * conv_state.shape[2]) // 3
    heads = attn_w // V_DIM
    pool_w = pool_state.shape[2]
    conv_dim = conv_state.shape[2]
    n_prev = 0 if k_prev is None else k_prev.shape[0]

    def tile(width):
        return pl.BlockSpec((nb, rows, width), lambda b, i: (b, i, 0))

    def layers_spec(n):
        return pl.BlockSpec((n, nb, rows * heads, V_DIM), lambda b, i: (0, b, i, 0))

    in_specs = [tile(d), _const_spec(w_in.shape),
                pl.BlockSpec((nb, POOL_HIST, pool_w), lambda b, i: (b, 0, 0)),
                pl.BlockSpec((nb, CONV_HIST, conv_dim), lambda b, i: (b, 0, 0)),
                _const_spec(pool_w_bd.shape), _const_spec(pool_scale.shape), _const_spec(conv_w.shape)]
    args = [x, w_in, pool_state, conv_state, pool_w_bd, pool_scale, conv_w]
    if n_prev:
        in_specs += [layers_spec(n_prev), layers_spec(n_prev)]
        args += [k_prev, v_prev]
    kv_shape = jax.ShapeDtypeStruct((n_prev + 1, bsz, s * heads, V_DIM), F32)
    dense = jax.ShapeDtypeStruct((bsz, s, attn_w), BF16)
    out_shape = (dense, dense, dense, kv_shape, kv_shape,
                 jax.ShapeDtypeStruct((bsz, s, pool_w + conv_dim), BF16),
                 jax.ShapeDtypeStruct((bsz, POOL_HIST, pool_w), F32),
                 jax.ShapeDtypeStruct((bsz, CONV_HIST, conv_dim), F32))
    out_specs = (tile(attn_w), tile(attn_w), tile(attn_w), layers_spec(n_prev + 1), layers_spec(n_prev + 1),
                 tile(pool_w + conv_dim),
                 pl.BlockSpec((nb, POOL_HIST, pool_w), lambda b, i: (b, 0, 0)),
                 pl.BlockSpec((nb, CONV_HIST, conv_dim), lambda b, i: (b, 0, 0)))
    return pl.pallas_call(
        functools.partial(_inproj_kernel, pos0, n_prev),
        out_shape=out_shape,
        grid=(bsz // nb, s // rows),
        in_specs=in_specs,
        out_specs=out_specs,
        scratch_shapes=[pltpu.VMEM((nb, POOL_HIST + rows, pool_w), F32),
                        pltpu.VMEM((nb, CONV_HIST + rows, conv_dim), F32)],
        compiler_params=_params(("arbitrary", "arbitrary")),
        name="inproj",
    )(*args)


def _bias_kernel(q0, k0, ref_bucket, rb_ref, o_ref):
    h = pl.program_id(0)
    _, nr, nc = o_ref.shape
    qp = q0 + lax.broadcasted_iota(jnp.int32, (nr, nc), 0)
    kp = k0 + lax.broadcasted_iota(jnp.int32, (nr, nc), 1)
    rel = kp - qp
    n = jnp.abs(rel)
    nb = NUM_BUCKETS // 2
    max_exact = nb // 2
    steps = nb - max_exact
    large = jnp.full((nr, nc), max_exact, jnp.int32)
    for j in range(1, steps):
        thr = math.ceil(max_exact * (MAX_DISTANCE / max_exact) ** (j / steps) - 1e-9)
        large = large + (n >= thr).astype(jnp.int32)
    bucket = jnp.where(rel > 0, nb, 0) + jnp.where(n < max_exact, n, large)
    val = jnp.zeros((nr, nc), F32)
    for bkt in range(NUM_BUCKETS):
        val = jnp.where(bucket == bkt, rb_ref[bkt, h], val)
    val = (val - rb_ref[ref_bucket, h]) * LOG2E
    visible = (kp // CHUNK) <= (qp // CHUNK)
    o_ref[0] = jnp.where(visible, val, NEG_INF)


def _bias_call(rel_bias, q0, k0, nr, nc):
    heads = rel_bias.shape[1]
    ref_bucket = NUM_BUCKETS // 2 - 1
    return pl.pallas_call(
        functools.partial(_bias_kernel, q0, k0, ref_bucket),
        out_shape=jax.ShapeDtypeStruct((heads, nr, nc), F32),
        grid=(heads,),
        in_specs=[pl.BlockSpec(memory_space=pltpu.SMEM)],
        out_specs=pl.BlockSpec((1, nr, nc), lambda h: (h, 0, 0)),
        compiler_params=_params(("arbitrary",)),
        name="relbias",
    )(rel_bias)


def _lambda_value(dl_ref, lam_init):
    dl = dl_ref[...]
    a = jnp.sum(dl[0:1] * dl[1:2], axis=-1, keepdims=True)
    b = jnp.sum(dl[2:3] * dl[3:4], axis=-1, keepdims=True)
    return jnp.exp(a) - jnp.exp(b) + lam_init


def _split_components(q):
    lane = lax.broadcasted_iota(jnp.int32, q.shape, 1)
    zero = jnp.zeros_like(q)
    return jnp.where(lane < QK_DIM, q, zero), jnp.where(lane >= QK_DIM, q, zero)


def _scores(qm, kt):
    return lax.dot_general(qm, kt, (((1,), (1,)), ((), ())), preferred_element_type=F32)


def _sub_norm(o, g, lam_init):
    o = o * lax.rsqrt(jnp.mean(o * o, axis=-1, keepdims=True) + RMS_EPS) * g
    return o * (1.0 - lam_init)


def _lane_blocks(s):
    return [s[:, n * LANES:(n + 1) * LANES] for n in range(s.shape[1] // LANES)]


def _tree(op, xs):
    while len(xs) > 1:
        xs = [op(xs[n], xs[n + 1]) if n + 1 < len(xs) else xs[n] for n in range(0, len(xs), 2)]
    return xs[0]


def _attn_kernel(lam_init, q_ref, k_ref, v_ref, d0_ref, d1_ref, dl_ref, g_ref, o_ref, m_ref, l_ref, acc_ref):
    i = pl.program_id(2)
    t = q_ref.shape[1]
    qms = _split_components(q_ref[0])
    m_ref[...] = jnp.full(m_ref.shape, NEG_INF, F32)
    l_ref[...] = jnp.zeros(l_ref.shape, F32)
    acc_ref[...] = jnp.zeros(acc_ref.shape, F32)

    def step(j, bias, tiles=1):
        off = pl.multiple_of(j * t, t)
        kt = k_ref[0, pl.ds(off, tiles * t), :]
        vt = v_ref[0, pl.ds(off, tiles * t), :]
        scores = [_scores(qm, kt) for qm in qms]
        if bias is not None:
            scores = [s + bias for s in scores]
        for c, s in enumerate(scores):
            blocks = _lane_blocks(s)
            m_prev = m_ref[c]
            m_cur = jnp.maximum(m_prev, jnp.max(_tree(jnp.maximum, blocks), axis=-1, keepdims=True))
            scale = jnp.exp2(m_prev - m_cur)
            ps = [jnp.exp2(blk - m_cur) for blk in blocks]
            l_ref[c] = scale * l_ref[c] + _tree(jnp.add, ps)
            p = jnp.concatenate([x.astype(BF16) for x in ps], axis=1)
            acc_ref[c] = scale * acc_ref[c] + jnp.dot(p, vt, preferred_element_type=F32)
            m_ref[c] = m_cur

    n_far = jnp.maximum(i - 1, 0)

    def far_quad(jj, carry):
        step(4 * jj, None, tiles=4)
        return carry

    lax.fori_loop(0, n_far // 4, far_quad, 0)

    @pl.when(n_far % 4 >= 2)
    def _():
        step(n_far // 4 * 4, None, tiles=2)

    @pl.when(n_far % 2 == 1)
    def _():
        step(n_far - 1, None)

    @pl.when(i > 0)
    def _():
        step(i - 1, jnp.concatenate([d1_ref[0], d0_ref[0]], axis=1), tiles=2)

    @pl.when(i == 0)
    def _():
        step(0, d0_ref[0])

    lam = _lambda_value(dl_ref, lam_init)
    outs = [acc_ref[c] / jnp.sum(l_ref[c], axis=-1, keepdims=True) for c in range(2)]
    o_ref[0] = _sub_norm(outs[0] - lam * outs[1], g_ref[...], lam_init).astype(BF16)


def _attn_call(q, k, v, d0, d1, dl, g, lam_init):
    bsz, s, width = q.shape
    heads = width // V_DIM
    t = min(ATTN_TILE, s)
    kv_spec = pl.BlockSpec((1, s, V_DIM), lambda b, h, i: (b, 0, h))
    tile_spec = pl.BlockSpec((1, t, V_DIM), lambda b, h, i: (b, i, h))
    return pl.pallas_call(
        functools.partial(_attn_kernel, lam_init),
        out_shape=jax.ShapeDtypeStruct((bsz, s, width), BF16),
        grid=(bsz, heads, s // t),
        in_specs=[tile_spec, kv_spec, kv_spec,
                  pl.BlockSpec((1, t, t), lambda b, h, i: (h, 0, 0)),
                  pl.BlockSpec((1, t, t), lambda b, h, i: (h, 0, 0)),
                  pl.BlockSpec(dl.shape, lambda b, h, i: (0, 0)),
                  pl.BlockSpec(g.shape, lambda b, h, i: (0, 0))],
        out_specs=tile_spec,
        scratch_shapes=[pltpu.VMEM((2, t, LANES), F32), pltpu.VMEM((2, t, LANES), F32),
                        pltpu.VMEM((2, t, V_DIM), F32)],
        compiler_params=_params(("arbitrary", "arbitrary", "arbitrary")),
        name="attn_prompt",
    )(q, k, v, d0, d1, dl, g)


def _attn_cache_kernel(lam_init, heads, q_ref, ck_ref, cv_ref, nk_ref, nv_ref, bn_ref, bw_ref, dl_ref, g_ref,
                       o_ref):
    past = ck_ref.shape[2] // heads
    far_n = past - NEAR_WINDOW
    lam = _lambda_value(dl_ref, lam_init)
    for hd in range(heads):
        lanes = slice(hd * V_DIM, (hd + 1) * V_DIM)
        kb = ck_ref[0, 0, pl.ds(hd, past, stride=heads), :].astype(BF16)
        vb = cv_ref[0, 0, pl.ds(hd, past, stride=heads), :].astype(BF16)
        nkb = nk_ref[0, :, lanes]
        nvb = nv_ref[0, :, lanes]
        outs = []
        for qm in _split_components(q_ref[0, :, lanes]):
            s_far = _scores(qm, kb[:far_n])
            s_near = _scores(qm, kb[far_n:]) + bn_ref[hd]
            s_new = _scores(qm, nkb) + bw_ref[hd]
            m = jnp.maximum(jnp.max(s_far, axis=-1, keepdims=True),
                            jnp.maximum(jnp.max(s_near, axis=-1, keepdims=True),
                                        jnp.max(s_new, axis=-1, keepdims=True)))
            p_far = jnp.exp2(s_far - m)
            p_near = jnp.exp2(s_near - m)
            p_new = jnp.exp2(s_new - m)
            denom = (jnp.sum(p_far, axis=-1, keepdims=True) + jnp.sum(p_near, axis=-1, keepdims=True)
                     + jnp.sum(p_new, axis=-1, keepdims=True))
            acc = jnp.dot(p_far.astype(BF16), vb[:far_n], preferred_element_type=F32)
            acc = acc + jnp.dot(p_near.astype(BF16), vb[far_n:], preferred_element_type=F32)
            acc = acc + jnp.dot(p_new.astype(BF16), nvb, preferred_element_type=F32)
            outs.append(acc / denom)
        o = outs[0] - lam * outs[1]
        o_ref[0, :, lanes] = _sub_norm(o, g_ref[...], lam_init).astype(BF16)


def _attn_cache_call(q, cache_k, cache_v, k_new, v_new, b_near, b_new, dl, g, layer, lam_init):
    bsz, s, width = q.shape
    heads = width // V_DIM
    cache_spec = pl.BlockSpec((1, 1) + cache_k.shape[2:], lambda b: (layer, b, 0, 0))
    tile_spec = pl.BlockSpec((1, s, width), lambda b: (b, 0, 0))
    return pl.pallas_call(
        functools.partial(_attn_cache_kernel, lam_init, heads),
        out_shape=jax.ShapeDtypeStruct((bsz, s, width), BF16),
        grid=(bsz,),
        in_specs=[tile_spec, cache_spec, cache_spec, tile_spec, tile_spec,
                  _const_spec(b_near.shape), _const_spec(b_new.shape), _const_spec(dl.shape), _const_spec(g.shape)],
        out_specs=tile_spec,
        compiler_params=_params(("arbitrary",)),
        name="attn_sample",
    )(q, cache_k, cache_v, k_new, v_new, b_near, b_new, dl, g)


def _block_diag(w):
    groups, c, _ = w.shape
    eye = jnp.eye(groups, dtype=w.dtype)
    return (eye[:, None, :, None] * w[:, :, None, :]).reshape(groups * c, groups * c)


def _pad_front(state, rows):
    return jnp.pad(state, ((0, 0), (rows - state.shape[1], 0), (0, 0)))


def _layer(x, lw, alpha, pool_state, conv_state, k_all, v_all, pos0, attend):
    bsz, s, d = x.shape
    x1 = _ffn_call(x.reshape(bsz * s, d), lw["wi1"], lw["wo1"], lw["g0"], lw["b0"], alpha)
    rows = min(TOKEN_TILE, s)
    q, kb, vb, k_all, v_all, pc, ptail, ctail = _inproj_call(
        x1.reshape(bsz, s, d), lw["w_in"], pool_state, conv_state, lw["pool_w"], lw["pool_scale"], lw["conv_w"],
        k_all, v_all, pos0, rows)
    attn = attend(q, kb, vb)
    x3 = _mixffn_call(x1, pc.reshape(bsz * s, -1), attn.reshape(bsz * s, -1), lw["w_out"], lw["g1"], lw["b1"],
                      lw["wi2"], lw["wo2"], lw["g2"], lw["b2"], alpha)
    return x3.reshape(bsz, s, d), k_all, v_all, ptail, ctail


def kernel(x_prompt, x_sample, cache_k, cache_v, state_pool, state_conv, ln_g, ln_b, w_ffn_in, w_ffn_out,
           w_in, w_out, pool_w, pool_scale, conv_w, diff_lambda, subln_g, rel_bias):
    depth = w_in.shape[0]
    bp, sp, d = x_prompt.shape
    bs, ss, _ = x_sample.shape
    past = cache_k.shape[2]
    heads = cache_k.shape[3]
    alpha = (2 * depth) ** 0.25
    t = min(ATTN_TILE, sp)

    d0 = _bias_call(rel_bias, 0, 0, t, t)
    d1 = _bias_call(rel_bias, t, 0, t, t)
    b_near = _bias_call(rel_bias, past, past - NEAR_WINDOW, ss, NEAR_WINDOW)
    b_new = _bias_call(rel_bias, past, past, ss, ss)

    ck = cache_k.reshape(depth, bs, past * heads, V_DIM)
    cv = cache_v.reshape(depth, bs, past * heads, V_DIM)
    zeros_pool = jnp.zeros((bp, POOL_HIST, state_pool.shape[-1]), F32)
    zeros_conv = jnp.zeros((bp, CONV_HIST, state_conv.shape[-1]), F32)

    hp, hs = x_prompt, x_sample
    kp = vp = ks = vs = None
    tails = {"pp": [], "cp": [], "ps": [], "cs": []}
    for l in range(depth):
        lam_init = 0.8 - 0.6 * math.exp(-0.3 * l)
        lw = {
            "wi1": w_ffn_in[l, 0].astype(BF16), "wo1": w_ffn_out[l, 0].astype(BF16),
            "wi2": w_ffn_in[l, 1].astype(BF16), "wo2": w_ffn_out[l, 1].astype(BF16),
            "w_in": w_in[l].astype(BF16), "w_out": w_out[l].astype(BF16),
            "pool_w": _block_diag(pool_w[l]).astype(BF16),
            "pool_scale": pool_scale[l][None, :], "conv_w": conv_w[l],
            "g0": ln_g[l, 0][None, :], "b0": ln_b[l, 0][None, :],
            "g1": ln_g[l, 1][None, :], "b1": ln_b[l, 1][None, :],
            "g2": ln_g[l, 2][None, :], "b2": ln_b[l, 2][None, :],
        }
        dl = diff_lambda[l]
        g = subln_g[l][None, :]

        def attend_prompt(q, kb, vb, lam_init=lam_init, dl=dl, g=g):
            return _attn_call(q, kb, vb, d0, d1, dl, g, lam_init)

        def attend_sample(q, kb, vb, l=l, lam_init=lam_init, dl=dl, g=g):
            return _attn_cache_call(q, ck, cv, kb, vb, b_near, b_new, dl, g, l, lam_init)

        hp, kp, vp, pt, ct = _layer(hp, lw, alpha, zeros_pool, zeros_conv, kp, vp, 0, attend_prompt)
        tails["pp"].append(pt)
        tails["cp"].append(ct)
        hs, ks, vs, pt, ct = _layer(hs, lw, alpha, _pad_front(state_pool[l], POOL_HIST),
                                    _pad_front(state_conv[l], CONV_HIST), ks, vs, past, attend_sample)
        tails["ps"].append(pt)
        tails["cs"].append(ct)

    def states(parts, keep):
        return jnp.stack(parts)[:, :, -keep:, :]

    return (hp, hs,
            kp.reshape(depth, bp, sp, heads, V_DIM), vp.reshape(depth, bp, sp, heads, V_DIM),
            states(tails["pp"], POOL_STATE), states(tails["cp"], CONV_K - 1),
            ks.reshape(depth, bs, ss, heads, V_DIM), vs.reshape(depth, bs, ss, heads, V_DIM),
            states(tails["ps"], POOL_STATE), states(tails["cs"], CONV_K - 1))
```

```python
import functools
import math

import jax
import jax.numpy as jnp
from jax import lax
from jax.experimental import pallas as pl
from jax.experimental.pallas import tpu as pltpu

F32 = jnp.float32
BF16 = jnp.bfloat16

CHUNK = 64
POOL_WINDOWS = (2, 4, 8, 16)
POOL_STATE = max(POOL_WINDOWS) - 1
CONV_K = 3
QK_DIM = 64
V_DIM = 2 * QK_DIM
NUM_BUCKETS = 32
MAX_DISTANCE = 128
LN_EPS = 1e-5
RMS_EPS = 1e-5
NEG_INF = -1e30
LOG2E = math.log2(math.e)

LANES = 128
SUBLANES = 8
MXU_DIM = 256
VMEM_LIMIT = 56 * 1024 * 1024

TOKEN_TILE = 1024
ATTN_TILE = 512
FF_CHUNK = MXU_DIM
POOL_HIST = 16
CONV_HIST = 8
NEAR_WINDOW = 256


def _const_spec(shape):
    nd = len(shape)
    return pl.BlockSpec(shape, lambda *_: (0,) * nd, pipeline_mode=pl.Buffered(1))


def _params(sem):
    return pltpu.CompilerParams(dimension_semantics=sem, vmem_limit_bytes=VMEM_LIMIT)


def _lane_blocks(s):
    return [s[:, n * LANES:(n + 1) * LANES] for n in range(s.shape[1] // LANES)]


def _tree(op, xs):
    while len(xs) > 1:
        xs = [op(xs[n], xs[n + 1]) if n + 1 < len(xs) else xs[n] for n in range(0, len(xs), 2)]
    return xs[0]


def _layer_norm(z, g, b):
    mu = jnp.mean(z, axis=-1, keepdims=True)
    zc = z - mu
    var = jnp.mean(zc * zc, axis=-1, keepdims=True)
    return zc * lax.rsqrt(var + LN_EPS) * g + b


def _swiglu(x, wi_ref, wo_ref, h_ref):
    d_ff = wo_ref.shape[0]
    xb = x.astype(BF16)
    for c in range(d_ff // FF_CHUNK):
        lo = c * FF_CHUNK
        gate = jnp.dot(xb, wi_ref[:, lo:lo + FF_CHUNK], preferred_element_type=F32)
        up = jnp.dot(xb, wi_ref[:, d_ff + lo:d_ff + lo + FF_CHUNK], preferred_element_type=F32)
        h_ref[:, lo:lo + FF_CHUNK] = (gate * jax.nn.sigmoid(gate) * up).astype(BF16)
    return jnp.dot(h_ref[...], wo_ref[...], preferred_element_type=F32)


def _ffn_kernel(alpha, x_ref, wi_ref, wo_ref, g_ref, b_ref, o_ref, h_ref):
    x = x_ref[...]
    y = _swiglu(x, wi_ref, wo_ref, h_ref)
    o_ref[...] = _layer_norm(alpha * x + 0.5 * y, g_ref[...], b_ref[...])


def _mixffn_kernel(alpha, x_ref, pc_ref, at_ref, wm_ref, g1_ref, b1_ref,
                   wi_ref, wo_ref, g2_ref, b2_ref, o_ref, h_ref):
    n_pc = pc_ref.shape[1]
    mix = jnp.dot(pc_ref[...], wm_ref[0:n_pc, :], preferred_element_type=F32)
    mix = mix + jnp.dot(at_ref[...], wm_ref[n_pc:, :], preferred_element_type=F32)
    x = _layer_norm(alpha * x_ref[...] + mix, g1_ref[...], b1_ref[...])
    y = _swiglu(x, wi_ref, wo_ref, h_ref)
    o_ref[...] = _layer_norm(alpha * x + 0.5 * y, g2_ref[...], b2_ref[...])


def _ffn_call(x, wi, wo, g, b, alpha):
    m, d = x.shape
    tm = min(TOKEN_TILE, m)
    d_ff = wo.shape[0]
    row = pl.BlockSpec((tm, d), lambda i: (i, 0))
    return pl.pallas_call(
        functools.partial(_ffn_kernel, alpha),
        out_shape=jax.ShapeDtypeStruct((m, d), F32),
        grid=(m // tm,),
        in_specs=[row, _const_spec(wi.shape), _const_spec(wo.shape),
                  _const_spec(g.shape), _const_spec(b.shape)],
        out_specs=row,
        scratch_shapes=[pltpu.VMEM((tm, d_ff), BF16)],
        compiler_params=_params(("arbitrary",)),
        name="ffn",
    )(x, wi, wo, g, b)


def _mixffn_call(x, pc, at, wm, g1, b1, wi, wo, g2, b2, alpha):
    m, d = x.shape
    tm = min(TOKEN_TILE, m)
    d_ff = wo.shape[0]
    row = pl.BlockSpec((tm, d), lambda i: (i, 0))
    half = pl.BlockSpec((tm, pc.shape[1]), lambda i: (i, 0))
    half2 = pl.BlockSpec((tm, at.shape[1]), lambda i: (i, 0))
    return pl.pallas_call(
        functools.partial(_mixffn_kernel, alpha),
        out_shape=jax.ShapeDtypeStruct((m, d), F32),
        grid=(m // tm,),
        in_specs=[row, half, half2, _const_spec(wm.shape), _const_spec(g1.shape), _const_spec(b1.shape),
                  _const_spec(wi.shape), _const_spec(wo.shape), _const_spec(g2.shape), _const_spec(b2.shape)],
        out_specs=row,
        scratch_shapes=[pltpu.VMEM((tm, d_ff), BF16)],
        compiler_params=_params(("arbitrary",)),
        name="mixffn",
    )(x, pc, at, wm, g1, b1, wi, wo, g2, b2)


def _inproj_kernel(pos0, n_prev, x_ref, w_ref, pst_ref, cst_ref, pw_ref, ps_ref, cw_ref, *refs):
    prev_refs, refs = refs[:2 if n_prev else 0], refs[2 if n_prev else 0:]
    q_ref, kb_ref, vb_ref, k_ref, v_ref, pc_ref, pt_ref, ct_ref, pext_ref, cext_ref = refs
    nb, rows, d = x_ref.shape
    pool_w = pst_ref.shape[2]
    conv_w = cst_ref.shape[2]
    attn_w = q_ref.shape[2]
    heads = attn_w // V_DIM
    i = pl.program_id(1)

    @pl.when(i == 0)
    def _():
        pext_ref[:, 0:POOL_HIST, :] = pst_ref[...]
        cext_ref[:, 0:CONV_HIST, :] = cst_ref[...]

    @pl.when(i > 0)
    def _():
        pext_ref[:, 0:POOL_HIST, :] = pext_ref[:, rows:rows + POOL_HIST, :]
        cext_ref[:, 0:CONV_HIST, :] = cext_ref[:, rows:rows + CONV_HIST, :]

    xb = x_ref[...].reshape(nb * rows, d).astype(BF16)

    def proj(lo, n):
        return jnp.dot(xb, w_ref[:, lo:lo + n], preferred_element_type=F32)

    off = 0
    u = proj(off, pool_w); off += pool_w
    b_gate = proj(off, conv_w); off += conv_w
    c_gate = proj(off, conv_w); off += conv_w
    h_conv = proj(off, conv_w); off += conv_w
    pext_ref[:, POOL_HIST:, :] = u.reshape(nb, rows, pool_w)
    cext_ref[:, CONV_HIST:, :] = (c_gate * h_conv).reshape(nb, rows, conv_w)

    group = pool_w // len(POOL_WINDOWS)
    lane = lax.broadcasted_iota(jnp.int32, (1, pool_w), 1)
    pos = pos0 + i * rows + lax.broadcasted_iota(jnp.int32, (rows, 1), 0)
    cnt = [jnp.minimum(w, pos + 1).astype(F32) for w in POOL_WINDOWS]
    count = jnp.where(lane < group, cnt[0],
                      jnp.where(lane < 2 * group, cnt[1],
                                jnp.where(lane < 3 * group, cnt[2], cnt[3])))
    for n in range(nb):
        e = pext_ref[n]
        s2 = e + pltpu.roll(e, 1, 0)
        s4 = s2 + pltpu.roll(s2, 2, 0)
        s8 = s4 + pltpu.roll(s4, 4, 0)
        s16 = s8 + pltpu.roll(s8, 8, 0)
        win = jnp.where(lane < group, s2,
                        jnp.where(lane < 2 * group, s4,
                                  jnp.where(lane < 3 * group, s8, s16)))[POOL_HIST:]
        diff = (win / count - e[POOL_HIST:]).astype(BF16)
        pooled = jnp.dot(diff, pw_ref[...], preferred_element_type=F32) * ps_ref[...]
        pc_ref[n, :, 0:pool_w] = pooled.astype(BF16)

        z = cext_ref[n]
        conv = cw_ref[2:3, :] * z + cw_ref[1:2, :] * pltpu.roll(z, 1, 0) + cw_ref[0:1, :] * pltpu.roll(z, 2, 0)
        gated = b_gate[n * rows:(n + 1) * rows] * conv[CONV_HIST:]
        pc_ref[n, :, pool_w:pool_w + conv_w] = gated.astype(BF16)

    pt_ref[...] = pext_ref[:, rows:rows + POOL_HIST, :]
    ct_ref[...] = cext_ref[:, rows:rows + CONV_HIST, :]

    q = proj(off, attn_w); off += attn_w
    q_ref[...] = (q * (QK_DIM ** -0.5 * LOG2E)).astype(BF16).reshape(nb, rows, attn_w)
    for idx, (dense_ref, out_ref) in enumerate(((kb_ref, k_ref), (vb_ref, v_ref))):
        kv = proj(off, attn_w); off += attn_w
        dense_ref[...] = kv.astype(BF16).reshape(nb, rows, attn_w)
        if n_prev:
            out_ref[0:n_prev] = prev_refs[idx][...]
        for n in range(nb):
            for hd in range(heads):
                out_ref[n_prev, n, pl.ds(hd, rows, stride=heads), :] = kv[n * rows:(n + 1) * rows,
                                                                          hd * V_DIM:(hd + 1) * V_DIM]


def _inproj_call(x, w_in, pool_state, conv_state, pool_w_bd, pool_scale, conv_w, k_prev, v_prev, pos0, rows):
    bsz, s, d = x.shape
    nb = bsz if rows == s and bsz * s <= TOKEN_TILE else 1
    attn_w = (w_in.shape[1] - pool_state.shape[2] - 3 * conv_state.shape[2]) // 3
    heads = attn_w // V_DIM
    pool_w = pool_state.shape[2]
    conv_dim = conv_state.shape[2]
    n_prev = 0 if k_prev is None else k_prev.shape[0]

    def tile(width):
        return pl.BlockSpec((nb, rows, width), lambda b, i: (b, i, 0))

    def layers_spec(n):
        return pl.BlockSpec((n, nb, rows * heads, V_DIM), lambda b, i: (0, b, i, 0))

    in_specs = [tile(d), _const_spec(w_in.shape),
                pl.BlockSpec((nb, POOL_HIST, pool_w), lambda b, i: (b, 0, 0)),
                pl.BlockSpec((nb, CONV_HIST, conv_dim), lambda b, i: (b, 0, 0)),
                _const_spec(pool_w_bd.shape), _const_spec(pool_scale.shape), _const_spec(conv_w.shape)]
    args = [x, w_in, pool_state, conv_state, pool_w_bd, pool_scale, conv_w]
    if n_prev:
        in_specs += [layers_spec(n_prev), layers_spec(n_prev)]
        args += [k_prev, v_prev]
    kv_shape = jax.ShapeDtypeStruct((n_prev + 1, bsz, s * heads, V_DIM), F32)
    dense = jax.ShapeDtypeStruct((bsz, s, attn_w), BF16)
    out_shape = (dense, dense, dense, kv_shape, kv_shape,
                 jax.ShapeDtypeStruct((bsz, s, pool_w + conv_dim), BF16),
                 jax.ShapeDtypeStruct((bsz, POOL_HIST, pool_w), F32),
                 jax.ShapeDtypeStruct((bsz, CONV_HIST, conv_dim), F32))
    out_specs = (tile(attn_w), tile(attn_w), tile(attn_w), layers_spec(n_prev + 1), layers_spec(n_prev + 1),
                 tile(pool_w + conv_dim),
                 pl.BlockSpec((nb, POOL_HIST, pool_w), lambda b, i: (b, 0, 0)),
                 pl.BlockSpec((nb, CONV_HIST, conv_dim), lambda b, i: (b, 0, 0)))
    return pl.pallas_call(
        functools.partial(_inproj_kernel, pos0, n_prev),
        out_shape=out_shape,
        grid=(bsz // nb, s // rows),
        in_specs=in_specs,
        out_specs=out_specs,
        scratch_shapes=[pltpu.VMEM((nb, POOL_HIST + rows, pool_w), F32),
                        pltpu.VMEM((nb, CONV_HIST + rows, conv_dim), F32)],
        compiler_params=_params(("arbitrary", "arbitrary")),
        name="inproj",
    )(*args)


def _bias_kernel(q0, k0, ref_bucket, rb_ref, o_ref):
    h = pl.program_id(0)
    _, nr, nc = o_ref.shape
    qp = q0 + lax.broadcasted_iota(jnp.int32, (nr, nc), 0)
    kp = k0 + lax.broadcasted_iota(jnp.int32, (nr, nc), 1)
    rel = kp - qp
    n = jnp.abs(rel)
    nb = NUM_BUCKETS // 2
    max_exact = nb // 2
    steps = nb - max_exact
    large = jnp.full((nr, nc), max_exact, jnp.int32)
    for j in range(1, steps):
        thr = math.ceil(max_exact * (MAX_DISTANCE / max_exact) ** (j / steps) - 1e-9)
        large = large + (n >= thr).astype(jnp.int32)
    bucket = jnp.where(rel > 0, nb, 0) + jnp.where(n < max_exact, n, large)
    val = jnp.zeros((nr, nc), F32)
    for bkt in range(NUM_BUCKETS):
        val = jnp.where(bucket == bkt, rb_ref[bkt, h], val)
    val = (val - rb_ref[ref_bucket, h]) * LOG2E
    visible = (kp // CHUNK) <= (qp // CHUNK)
    o_ref[0] = jnp.where(visible, val, NEG_INF)


def _bias_call(rel_bias, q0, k0, nr, nc):
    heads = rel_bias.shape[1]
    ref_bucket = NUM_BUCKETS // 2 - 1
    return pl.pallas_call(
        functools.partial(_bias_kernel, q0, k0, ref_bucket),
        out_shape=jax.ShapeDtypeStruct((heads, nr, nc), F32),
        grid=(heads,),
        in_specs=[pl.BlockSpec(memory_space=pltpu.SMEM)],
        out_specs=pl.BlockSpec((1, nr, nc), lambda h: (h, 0, 0)),
        compiler_params=_params(("arbitrary",)),
        name="relbias",
    )(rel_bias)


def _lambda_value(dl_ref, lam_init):
    dl = dl_ref[...]
    a = jnp.sum(dl[0:1] * dl[1:2], axis=-1, keepdims=True)
    b = jnp.sum(dl[2:3] * dl[3:4], axis=-1, keepdims=True)
    return jnp.exp(a) - jnp.exp(b) + lam_init


def _split_components(q):
    lane = lax.broadcasted_iota(jnp.int32, q.shape, 1)
    zero = jnp.zeros_like(q)
    return jnp.where(lane < QK_DIM, q, zero), jnp.where(lane >= QK_DIM, q, zero)


def _scores(qm, kt):
    return lax.dot_general(qm, kt, (((1,), (1,)), ((), ())), preferred_element_type=F32)


def _sub_norm(o, g, lam_init):
    o = o * lax.rsqrt(jnp.mean(o * o, axis=-1, keepdims=True) + RMS_EPS) * g
    return o * (1.0 - lam_init)


def _attn_kernel(lam_init, q_ref, k_ref, v_ref, d0_ref, d1_ref, dl_ref, g_ref, o_ref, m_ref, l_ref, acc_ref):
    i = pl.program_id(2)
    t = q_ref.shape[1]
    qms = _split_components(q_ref[0])
    m_ref[...] = jnp.full(m_ref.shape, NEG_INF, F32)
    l_ref[...] = jnp.zeros(l_ref.shape, F32)
    acc_ref[...] = jnp.zeros(acc_ref.shape, F32)

    def step(j, bias, tiles=1):
        off = pl.multiple_of(j * t, t)
        kt = k_ref[0, pl.ds(off, tiles * t), :]
        vt = v_ref[0, pl.ds(off, tiles * t), :]
        scores = [_scores(qm, kt) for qm in qms]
        if bias is not None:
            scores = [s + bias for s in scores]
        for c, s in enumerate(scores):
            blocks = _lane_blocks(s)
            m_prev = m_ref[c]
            m_cur = jnp.maximum(m_prev, jnp.max(_tree(jnp.maximum, blocks), axis=-1, keepdims=True))
            scale = jnp.exp2(m_prev - m_cur)
            ps = [jnp.exp2(blk - m_cur) for blk in blocks]
            l_ref[c] = scale * l_ref[c] + _tree(jnp.add, ps)
            p = jnp.concatenate([x.astype(BF16) for x in ps], axis=1)
            acc_ref[c] = scale * acc_ref[c] + jnp.dot(p, vt, preferred_element_type=F32)
            m_ref[c] = m_cur

    n_far = jnp.maximum(i - 1, 0)

    def far_quad(jj, carry):
        step(4 * jj, None, tiles=4)
        return carry

    lax.fori_loop(0, n_far // 4, far_quad, 0)

    @pl.when(n_far % 4 >= 2)
    def _():
        step(n_far // 4 * 4, None, tiles=2)

    @pl.when(n_far % 2 == 1)
    def _():
        step(n_far - 1, None)

    @pl.when(i > 0)
    def _():
        step(i - 1, jnp.concatenate([d1_ref[0], d0_ref[0]], axis=1), tiles=2)

    @pl.when(i == 0)
    def _():
        step(0, d0_ref[0])

    lam = _lambda_value(dl_ref, lam_init)
    outs = [acc_ref[c] / jnp.sum(l_ref[c], axis=-1, keepdims=True) for c in range(2)]
    o_ref[0] = _sub_norm(outs[0] - lam * outs[1], g_ref[...], lam_init).astype(BF16)


def _attn_call(q, k, v, d0, d1, dl, g, lam_init):
    bsz, s, width = q.shape
    heads = width // V_DIM
    t = min(ATTN_TILE, s)
    kv_spec = pl.BlockSpec((1, s, V_DIM), lambda b, h, i: (b, 0, h))
    tile_spec = pl.BlockSpec((1, t, V_DIM), lambda b, h, i: (b, i, h))
    return pl.pallas_call(
        functools.partial(_attn_kernel, lam_init),
        out_shape=jax.ShapeDtypeStruct((bsz, s, width), BF16),
        grid=(bsz, heads, s // t),
        in_specs=[tile_spec, kv_spec, kv_spec,
                  pl.BlockSpec((1, t, t), lambda b, h, i: (h, 0, 0)),
                  pl.BlockSpec((1, t, t), lambda b, h, i: (h, 0, 0)),
                  pl.BlockSpec(dl.shape, lambda b, h, i: (0, 0)),
                  pl.BlockSpec(g.shape, lambda b, h, i: (0, 0))],
        out_specs=tile_spec,
        scratch_shapes=[pltpu.VMEM((2, t, LANES), F32), pltpu.VMEM((2, t, LANES), F32),
                        pltpu.VMEM((2, t, V_DIM), F32)],
        compiler_params=_params(("arbitrary", "arbitrary", "arbitrary")),
        name="attn_prompt",
    )(q, k, v, d0, d1, dl, g)


def _attn_cache_kernel(lam_init, heads, q_ref, ck_ref, cv_ref, nk_ref, nv_ref, bn_ref, bw_ref, dl_ref, g_ref,
                       o_ref):
    past = ck_ref.shape[2] // heads
    far_n = past - NEAR_WINDOW
    lam = _lambda_value(dl_ref, lam_init)
    for hd in range(heads):
        lanes = slice(hd * V_DIM, (hd + 1) * V_DIM)
        kb = ck_ref[0, 0, pl.ds(hd, past, stride=heads), :].astype(BF16)
        vb = cv_ref[0, 0, pl.ds(hd, past, stride=heads), :].astype(BF16)
        nkb = nk_ref[0, :, lanes]
        nvb = nv_ref[0, :, lanes]
        outs = []
        for qm in _split_components(q_ref[0, :, lanes]):
            s_far = _scores(qm, kb[:far_n])
            s_near = _scores(qm, kb[far_n:]) + bn_ref[hd]
            s_new = _scores(qm, nkb) + bw_ref[hd]
            m = jnp.maximum(jnp.max(s_far, axis=-1, keepdims=True),
                            jnp.maximum(jnp.max(s_near, axis=-1, keepdims=True),
                                        jnp.max(s_new, axis=-1, keepdims=True)))
            p_far = jnp.exp2(s_far - m)
            p_near = jnp.exp2(s_near - m)
            p_new = jnp.exp2(s_new - m)
            denom = (jnp.sum(p_far, axis=-1, keepdims=True) + jnp.sum(p_near, axis=-1, keepdims=True)
                     + jnp.sum(p_new, axis=-1, keepdims=True))
            acc = jnp.dot(p_far.astype(BF16), vb[:far_n], preferred_element_type=F32)
            acc = acc + jnp.dot(p_near.astype(BF16), vb[far_n:], preferred_element_type=F32)
            acc = acc + jnp.dot(p_new.astype(BF16), nvb, preferred_element_type=F32)
            outs.append(acc / denom)
        o = outs[0] - lam * outs[1]
        o_ref[0, :, lanes] = _sub_norm(o, g_ref[...], lam_init).astype(BF16)


def _attn_cache_call(q, cache_k, cache_v, k_new, v_new, b_near, b_new, dl, g, layer, lam_init):
    bsz, s, width = q.shape
    heads = width // V_DIM
    cache_spec = pl.BlockSpec((1, 1) + cache_k.shape[2:], lambda b: (layer, b, 0, 0))
    tile_spec = pl.BlockSpec((1, s, width), lambda b: (b, 0, 0))
    return pl.pallas_call(
        functools.partial(_attn_cache_kernel, lam_init, heads),
        out_shape=jax.ShapeDtypeStruct((bsz, s, width), BF16),
        grid=(bsz,),
        in_specs=[tile_spec, cache_spec, cache_spec, tile_spec, tile_spec,
                  _const_spec(b_near.shape), _const_spec(b_new.shape), _const_spec(dl.shape), _const_spec(g.shape)],
        out_specs=tile_spec,
        compiler_params=_params(("arbitrary",)),
        name="attn_sample",
    )(q, cache_k, cache_v, k_new, v_new, b_near, b_new, dl, g)


def _block_diag(w):
    groups, c, _ = w.shape
    eye = jnp.eye(groups, dtype=w.dtype)
    return (eye[:, None, :, None] * w[:, :, None, :]).reshape(groups * c, groups * c)


def _pad_front(state, rows):
    return jnp.pad(state, ((0, 0), (rows - state.shape[1], 0), (0, 0)))


def _layer(x, lw, alpha, pool_state, conv_state, k_all, v_all, pos0, attend):
    bsz, s, d = x.shape
    x1 = _ffn_call(x.reshape(bsz * s, d), lw["wi1"], lw["wo1"], lw["g0"], lw["b0"], alpha)
    rows = min(TOKEN_TILE, s)
    q, kb, vb, k_all, v_all, pc, ptail, ctail = _inproj_call(
        x1.reshape(bsz, s, d), lw["w_in"], pool_state, conv_state, lw["pool_w"], lw["pool_scale"], lw["conv_w"],
        k_all, v_all, pos0, rows)
    attn = attend(q, kb, vb)
    x3 = _mixffn_call(x1, pc.reshape(bsz * s, -1), attn.reshape(bsz * s, -1), lw["w_out"], lw["g1"], lw["b1"],
                      lw["wi2"], lw["wo2"], lw["g2"], lw["b2"], alpha)
    return x3.reshape(bsz, s, d), k_all, v_all, ptail, ctail


def kernel(x_prompt, x_sample, cache_k, cache_v, state_pool, state_conv, ln_g, ln_b, w_ffn_in, w_ffn_out,
           w_in, w_out, pool_w, pool_scale, conv_w, diff_lambda, subln_g, rel_bias):
    depth = w_in.shape[0]
    bp, sp, d = x_prompt.shape
    bs, ss, _ = x_sample.shape
    past = cache_k.shape[2]
    heads = cache_k.shape[3]
    alpha = (2 * depth) ** 0.25
    t = min(ATTN_TILE, sp)

    d0 = _bias_call(rel_bias, 0, 0, t, t)
    d1 = _bias_call(rel_bias, t, 0, t, t)
    b_near = _bias_call(rel_bias, past, past - NEAR_WINDOW, ss, NEAR_WINDOW)
    b_new = _bias_call(rel_bias, past, past, ss, ss)

    ck = cache_k.reshape(depth, bs, past * heads, V_DIM)
    cv = cache_v.reshape(depth, bs, past * heads, V_DIM)
    zeros_pool = jnp.zeros((bp, POOL_HIST, state_pool.shape[-1]), F32)
    zeros_conv = jnp.zeros((bp, CONV_HIST, state_conv.shape[-1]), F32)

    hp, hs = x_prompt, x_sample
    kp = vp = ks = vs = None
    tails = {"pp": [], "cp": [], "ps": [], "cs": []}
    for l in range(depth):
        lam_init = 0.8 - 0.6 * math.exp(-0.3 * l)
        lw = {
            "wi1": w_ffn_in[l, 0].astype(BF16), "wo1": w_ffn_out[l, 0].astype(BF16),
            "wi2": w_ffn_in[l, 1].astype(BF16), "wo2": w_ffn_out[l, 1].astype(BF16),
            "w_in": w_in[l].astype(BF16), "w_out": w_out[l].astype(BF16),
            "pool_w": _block_diag(pool_w[l]).astype(BF16),
            "pool_scale": pool_scale[l][None, :], "conv_w": conv_w[l],
            "g0": ln_g[l, 0][None, :], "b0": ln_b[l, 0][None, :],
            "g1": ln_g[l, 1][None, :], "b1": ln_b[l, 1][None, :],
            "g2": ln_g[l, 2][None, :], "b2": ln_b[l, 2][None, :],
        }
        dl = diff_lambda[l]
        g = subln_g[l][None, :]

        def attend_prompt(q, kb, vb, lam_init=lam_init, dl=dl, g=g):
            return _attn_call(q, kb, vb, d0, d1, dl, g, lam_init)

        def attend_sample(q, kb, vb, l=l, lam_init=lam_init, dl=dl, g=g):
            return _attn_cache_call(q, ck, cv, kb, vb, b_near, b_new, dl, g, l, lam_init)

        hp, kp, vp, pt, ct = _layer(hp, lw, alpha, zeros_pool, zeros_conv, kp, vp, 0, attend_prompt)
        tails["pp"].append(pt)
        tails["cp"].append(ct)
        hs, ks, vs, pt, ct = _layer(hs, lw, alpha, _pad_front(state_pool[l], POOL_HIST),
                                    _pad_front(state_conv[l], CONV_HIST), ks, vs, past, attend_sample)
        tails["ps"].append(pt)
        tails["cs"].append(ct)

    def states(parts, keep):
        return jnp.stack(parts)[:, :, -keep:, :]

    return (hp, hs,
            kp.reshape(depth, bp, sp, heads, V_DIM), vp.reshape(depth, bp, sp, heads, V_DIM),
            states(tails["pp"], POOL_STATE), states(tails["cp"], CONV_K - 1),
            ks.reshape(depth, bs, ss, heads, V_DIM), vs.reshape(depth, bs, ss, heads, V_DIM),
            states(tails["ps"], POOL_STATE), states(tails["cs"], CONV_K - 1))
```

```python
import functools
import math

import jax
import jax.numpy as jnp
from jax import lax
from jax.experimental import pallas as pl
from jax.experimental.pallas import tpu as pltpu

F32 = jnp.float32
BF16 = jnp.bfloat16

CHUNK = 64
POOL_WINDOWS = (2, 4, 8, 16)
POOL_STATE = max(POOL_WINDOWS) - 1
CONV_K = 3
QK_DIM = 64
V_DIM = 2 * QK_DIM
NUM_BUCKETS = 32
MAX_DISTANCE = 128
LN_EPS = 1e-5
RMS_EPS = 1e-5
NEG_INF = -1e30
LOG2E = math.log2(math.e)

LANES = 128
SUBLANES = 8
MXU_DIM = 256
VMEM_LIMIT = 56 * 1024 * 1024

TOKEN_TILE = 1024
ATTN_TILE = 512
ATTN_TILES_PER_STEP = 4
FF_CHUNK = MXU_DIM
POOL_HIST = 16
CONV_HIST = 8
NEAR_WINDOW = 256


def _const_spec(shape):
    nd = len(shape)
    return pl.BlockSpec(shape, lambda *_: (0,) * nd, pipeline_mode=pl.Buffered(1))


def _params(sem):
    return pltpu.CompilerParams(dimension_semantics=sem, vmem_limit_bytes=VMEM_LIMIT)


def _lane_blocks(s):
    return [s[:, n * LANES:(n + 1) * LANES] for n in range(s.shape[1] // LANES)]


def _tree(op, xs):
    while len(xs) > 1:
        xs = [op(xs[n], xs[n + 1]) if n + 1 < len(xs) else xs[n] for n in range(0, len(xs), 2)]
    return xs[0]


def _layer_norm(z, g, b):
    mu = jnp.mean(z, axis=-1, keepdims=True)
    zc = z - mu
    var = jnp.mean(zc * zc, axis=-1, keepdims=True)
    return zc * lax.rsqrt(var + LN_EPS) * g + b


def _swiglu(x, wi_ref, wo_ref, h_ref):
    d_ff = wo_ref.shape[0]
    xb = x.astype(BF16)
    for c in range(d_ff // FF_CHUNK):
        lo = c * FF_CHUNK
        gate = jnp.dot(xb, wi_ref[:, lo:lo + FF_CHUNK], preferred_element_type=F32)
        up = jnp.dot(xb, wi_ref[:, d_ff + lo:d_ff + lo + FF_CHUNK], preferred_element_type=F32)
        h_ref[:, lo:lo + FF_CHUNK] = (gate * jax.nn.sigmoid(gate) * up).astype(BF16)
    return jnp.dot(h_ref[...], wo_ref[...], preferred_element_type=F32)


def _ffn_kernel(alpha, x_ref, wi_ref, wo_ref, g_ref, b_ref, o_ref, h_ref):
    x = x_ref[...]
    y = _swiglu(x, wi_ref, wo_ref, h_ref)
    o_ref[...] = _layer_norm(alpha * x + 0.5 * y, g_ref[...], b_ref[...])


def _mixffn_kernel(alpha, x_ref, pc_ref, at_ref, wm_ref, g1_ref, b1_ref,
                   wi_ref, wo_ref, g2_ref, b2_ref, o_ref, h_ref):
    n_pc = pc_ref.shape[1]
    mix = jnp.dot(pc_ref[...], wm_ref[0:n_pc, :], preferred_element_type=F32)
    mix = mix + jnp.dot(at_ref[...], wm_ref[n_pc:, :], preferred_element_type=F32)
    x = _layer_norm(alpha * x_ref[...] + mix, g1_ref[...], b1_ref[...])
    y = _swiglu(x, wi_ref, wo_ref, h_ref)
    o_ref[...] = _layer_norm(alpha * x + 0.5 * y, g2_ref[...], b2_ref[...])


def _ffn_call(x, wi, wo, g, b, alpha):
    m, d = x.shape
    tm = min(TOKEN_TILE, m)
    d_ff = wo.shape[0]
    row = pl.BlockSpec((tm, d), lambda i: (i, 0))
    return pl.pallas_call(
        functools.partial(_ffn_kernel, alpha),
        out_shape=jax.ShapeDtypeStruct((m, d), F32),
        grid=(m // tm,),
        in_specs=[row, _const_spec(wi.shape), _const_spec(wo.shape),
                  _const_spec(g.shape), _const_spec(b.shape)],
        out_specs=row,
        scratch_shapes=[pltpu.VMEM((tm, d_ff), BF16)],
        compiler_params=_params(("arbitrary",)),
        name="ffn",
    )(x, wi, wo, g, b)


def _mixffn_call(x, pc, at, wm, g1, b1, wi, wo, g2, b2, alpha):
    m, d = x.shape
    tm = min(TOKEN_TILE, m)
    d_ff = wo.shape[0]
    row = pl.BlockSpec((tm, d), lambda i: (i, 0))
    half = pl.BlockSpec((tm, pc.shape[1]), lambda i: (i, 0))
    half2 = pl.BlockSpec((tm, at.shape[1]), lambda i: (i, 0))
    return pl.pallas_call(
        functools.partial(_mixffn_kernel, alpha),
        out_shape=jax.ShapeDtypeStruct((m, d), F32),
        grid=(m // tm,),
        in_specs=[row, half, half2, _const_spec(wm.shape), _const_spec(g1.shape), _const_spec(b1.shape),
                  _const_spec(wi.shape), _const_spec(wo.shape), _const_spec(g2.shape), _const_spec(b2.shape)],
        out_specs=row,
        scratch_shapes=[pltpu.VMEM((tm, d_ff), BF16)],
        compiler_params=_params(("arbitrary",)),
        name="mixffn",
    )(x, pc, at, wm, g1, b1, wi, wo, g2, b2)


def _inproj_kernel(pos0, n_prev, x_ref, w_ref, pst_ref, cst_ref, pw_ref, ps_ref, cw_ref, *refs):
    prev_refs, refs = refs[:2 if n_prev else 0], refs[2 if n_prev else 0:]
    q_ref, kb_ref, vb_ref, k_ref, v_ref, pc_ref, pt_ref, ct_ref, pext_ref, cext_ref = refs
    nb, rows, d = x_ref.shape
    pool_w = pst_ref.shape[2]
    conv_w = cst_ref.shape[2]
    attn_w = q_ref.shape[2]
    heads = attn_w // V_DIM
    i = pl.program_id(1)

    @pl.when(i == 0)
    def _():
        pext_ref[:, 0:POOL_HIST, :] = pst_ref[...]
        cext_ref[:, 0:CONV_HIST, :] = cst_ref[...]

    @pl.when(i > 0)
    def _():
        pext_ref[:, 0:POOL_HIST, :] = pext_ref[:, rows:rows + POOL_HIST, :]
        cext_ref[:, 0:CONV_HIST, :] = cext_ref[:, rows:rows + CONV_HIST, :]

    xb = x_ref[...].reshape(nb * rows, d).astype(BF16)

    def proj(lo, n):
        return jnp.dot(xb, w_ref[:, lo:lo + n], preferred_element_type=F32)

    off = 0
    u = proj(off, pool_w); off += pool_w
    b_gate = proj(off, conv_w); off += conv_w
    c_gate = proj(off, conv_w); off += conv_w
    h_conv = proj(off, conv_w); off += conv_w
    pext_ref[:, POOL_HIST:, :] = u.reshape(nb, rows, pool_w)
    cext_ref[:, CONV_HIST:, :] = (c_gate * h_conv).reshape(nb, rows, conv_w)

    group = pool_w // len(POOL_WINDOWS)
    lane = lax.broadcasted_iota(jnp.int32, (1, pool_w), 1)
    pos = pos0 + i * rows + lax.broadcasted_iota(jnp.int32, (rows, 1), 0)
    cnt = [jnp.minimum(w, pos + 1).astype(F32) for w in POOL_WINDOWS]
    count = jnp.where(lane < group, cnt[0],
                      jnp.where(lane < 2 * group, cnt[1],
                                jnp.where(lane < 3 * group, cnt[2], cnt[3])))
    for n in range(nb):
        e = pext_ref[n]
        s2 = e + pltpu.roll(e, 1, 0)
        s4 = s2 + pltpu.roll(s2, 2, 0)
        s8 = s4 + pltpu.roll(s4, 4, 0)
        s16 = s8 + pltpu.roll(s8, 8, 0)
        win = jnp.where(lane < group, s2,
                        jnp.where(lane < 2 * group, s4,
                                  jnp.where(lane < 3 * group, s8, s16)))[POOL_HIST:]
        diff = (win / count - e[POOL_HIST:]).astype(BF16)
        pooled = jnp.dot(diff, pw_ref[...], preferred_element_type=F32) * ps_ref[...]
        pc_ref[n, :, 0:pool_w] = pooled.astype(BF16)

        z = cext_ref[n]
        conv = cw_ref[2:3, :] * z + cw_ref[1:2, :] * pltpu.roll(z, 1, 0) + cw_ref[0:1, :] * pltpu.roll(z, 2, 0)
        gated = b_gate[n * rows:(n + 1) * rows] * conv[CONV_HIST:]
        pc_ref[n, :, pool_w:pool_w + conv_w] = gated.astype(BF16)

    pt_ref[...] = pext_ref[:, rows:rows + POOL_HIST, :]
    ct_ref[...] = cext_ref[:, rows:rows + CONV_HIST, :]

    q = proj(off, attn_w); off += attn_w
    q_ref[...] = (q * (QK_DIM ** -0.5 * LOG2E)).astype(BF16).reshape(nb, rows, attn_w)
    for idx, (dense_ref, out_ref) in enumerate(((kb_ref, k_ref), (vb_ref, v_ref))):
        kv = proj(off, attn_w); off += attn_w
        dense_ref[...] = kv.astype(BF16).reshape(nb, rows, attn_w)
        if n_prev:
            out_ref[0:n_prev] = prev_refs[idx][...]
        for n in range(nb):
            for hd in range(heads):
                out_ref[n_prev, n, pl.ds(hd, rows, stride=heads), :] = kv[n * rows:(n + 1) * rows,
                                                                          hd * V_DIM:(hd + 1) * V_DIM]


def _inproj_call(x, w_in, pool_state, conv_state, pool_w_bd, pool_scale, conv_w, k_prev, v_prev, pos0, rows):
    bsz, s, d = x.shape
    nb = bsz if rows == s and bsz * s <= TOKEN_TILE else 1
    attn_w = (w_in.shape[1] - pool_state.shape[2] - 3 * conv_state.shape[2]) // 3
    heads = attn_w // V_DIM
    pool_w = pool_state.shape[2]
    conv_dim = conv_state.shape[2]
    n_prev = 0 if k_prev is None else k_prev.shape[0]

    def tile(width):
        return pl.BlockSpec((nb, rows, width), lambda b, i: (b, i, 0))

    def layers_spec(n):
        return pl.BlockSpec((n, nb, rows * heads, V_DIM), lambda b, i: (0, b, i, 0))

    in_specs = [tile(d), _const_spec(w_in.shape),
                pl.BlockSpec((nb, POOL_HIST, pool_w), lambda b, i: (b, 0, 0)),
                pl.BlockSpec((nb, CONV_HIST, conv_dim), lambda b, i: (b, 0, 0)),
                _const_spec(pool_w_bd.shape), _const_spec(pool_scale.shape), _const_spec(conv_w.shape)]
    args = [x, w_in, pool_state, conv_state, pool_w_bd, pool_scale, conv_w]
    if n_prev:
        in_specs += [layers_spec(n_prev), layers_spec(n_prev)]
        args += [k_prev, v_prev]
    kv_shape = jax.ShapeDtypeStruct((n_prev + 1, bsz, s * heads, V_DIM), F32)
    dense = jax.ShapeDtypeStruct((bsz, s, attn_w), BF16)
    out_shape = (dense, dense, dense, kv_shape, kv_shape,
                 jax.ShapeDtypeStruct((bsz, s, pool_w + conv_dim), BF16),
                 jax.ShapeDtypeStruct((bsz, POOL_HIST, pool_w), F32),
                 jax.ShapeDtypeStruct((bsz, CONV_HIST, conv_dim), F32))
    out_specs = (tile(attn_w), tile(attn_w), tile(attn_w), layers_spec(n_prev + 1), layers_spec(n_prev + 1),
                 tile(pool_w + conv_dim),
                 pl.BlockSpec((nb, POOL_HIST, pool_w), lambda b, i: (b, 0, 0)),
                 pl.BlockSpec((nb, CONV_HIST, conv_dim), lambda b, i: (b, 0, 0)))
    return pl.pallas_call(
        functools.partial(_inproj_kernel, pos0, n_prev),
        out_shape=out_shape,
        grid=(bsz // nb, s // rows),
        in_specs=in_specs,
        out_specs=out_specs,
        scratch_shapes=[pltpu.VMEM((nb, POOL_HIST + rows, pool_w), F32),
                        pltpu.VMEM((nb, CONV_HIST + rows, conv_dim), F32)],
        compiler_params=_params(("arbitrary", "arbitrary")),
        name="inproj",
    )(*args)


def _bias_kernel(q0, k0, ref_bucket, rb_ref, o_ref):
    h = pl.program_id(0)
    _, nr, nc = o_ref.shape
    qp = q0 + lax.broadcasted_iota(jnp.int32, (nr, nc), 0)
    kp = k0 + lax.broadcasted_iota(jnp.int32, (nr, nc), 1)
    rel = kp - qp
    n = jnp.abs(rel)
    nb = NUM_BUCKETS // 2
    max_exact = nb // 2
    steps = nb - max_exact
    large = jnp.full((nr, nc), max_exact, jnp.int32)
    for j in range(1, steps):
        thr = math.ceil(max_exact * (MAX_DISTANCE / max_exact) ** (j / steps) - 1e-9)
        large = large + (n >= thr).astype(jnp.int32)
    bucket = jnp.where(rel > 0, nb, 0) + jnp.where(n < max_exact, n, large)
    val = jnp.zeros((nr, nc), F32)
    for bkt in range(NUM_BUCKETS):
        val = jnp.where(bucket == bkt, rb_ref[bkt, h], val)
    val = (val - rb_ref[ref_bucket, h]) * LOG2E
    visible = (kp // CHUNK) <= (qp // CHUNK)
    o_ref[0] = jnp.where(visible, val, NEG_INF)


def _bias_call(rel_bias, q0, k0, nr, nc):
    heads = rel_bias.shape[1]
    ref_bucket = NUM_BUCKETS // 2 - 1
    return pl.pallas_call(
        functools.partial(_bias_kernel, q0, k0, ref_bucket),
        out_shape=jax.ShapeDtypeStruct((heads, nr, nc), F32),
        grid=(heads,),
        in_specs=[pl.BlockSpec(memory_space=pltpu.SMEM)],
        out_specs=pl.BlockSpec((1, nr, nc), lambda h: (h, 0, 0)),
        compiler_params=_params(("arbitrary",)),
        name="relbias",
    )(rel_bias)


def _lambda_value(dl_ref, lam_init):
    dl = dl_ref[...]
    a = jnp.sum(dl[0:1] * dl[1:2], axis=-1, keepdims=True)
    b = jnp.sum(dl[2:3] * dl[3:4], axis=-1, keepdims=True)
    return jnp.exp(a) - jnp.exp(b) + lam_init


def _split_components(q):
    lane = lax.broadcasted_iota(jnp.int32, q.shape, 1)
    zero = jnp.zeros_like(q)
    return jnp.where(lane < QK_DIM, q, zero), jnp.where(lane >= QK_DIM, q, zero)


def _scores(qm, kt):
    return lax.dot_general(qm, kt, (((1,), (1,)), ((), ())), preferred_element_type=F32)


def _sub_norm(o, g, lam_init):
    o = o * lax.rsqrt(jnp.mean(o * o, axis=-1, keepdims=True) + RMS_EPS) * g
    return o * (1.0 - lam_init)


def _attn_kernel(lam_init, q_ref, k_ref, v_ref, d0_ref, d1_ref, dl_ref, g_ref, o_ref, m_ref, l_ref, acc_ref):
    t = d0_ref.shape[1]
    per_step = q_ref.shape[1] // t
    lam = _lambda_value(dl_ref, lam_init)
    for sub in range(per_step):
        _attn_query_tile(pl.program_id(2) * per_step + sub, slice(sub * t, (sub + 1) * t), lam, lam_init,
                         q_ref, k_ref, v_ref, d0_ref, d1_ref, g_ref, o_ref, m_ref, l_ref, acc_ref)


def _attn_query_tile(i, rows, lam, lam_init, q_ref, k_ref, v_ref, d0_ref, d1_ref, g_ref, o_ref,
                     m_ref, l_ref, acc_ref):
    t = d0_ref.shape[1]
    qms = _split_components(q_ref[0, rows, :])
    m_ref[...] = jnp.full(m_ref.shape, NEG_INF, F32)
    l_ref[...] = jnp.zeros(l_ref.shape, F32)
    acc_ref[...] = jnp.zeros(acc_ref.shape, F32)

    def step(j, bias, tiles=1):
        off = pl.multiple_of(j * t, t)
        kt = k_ref[0, pl.ds(off, tiles * t), :]
        vt = v_ref[0, pl.ds(off, tiles * t), :]
        scores = [_scores(qm, kt) for qm in qms]
        if bias is not None:
            scores = [s + bias for s in scores]
        for c, s in enumerate(scores):
            blocks = _lane_blocks(s)
            m_prev = m_ref[c]
            m_cur = jnp.maximum(m_prev, jnp.max(_tree(jnp.maximum, blocks), axis=-1, keepdims=True))
            scale = jnp.exp2(m_prev - m_cur)
            ps = [jnp.exp2(blk - m_cur) for blk in blocks]
            l_ref[c] = scale * l_ref[c] + _tree(jnp.add, ps)
            p = jnp.concatenate([x.astype(BF16) for x in ps], axis=1)
            acc_ref[c] = scale * acc_ref[c] + jnp.dot(p, vt, preferred_element_type=F32)
            m_ref[c] = m_cur

    n_far = jnp.maximum(i - 1, 0)

    def far_quad(jj, carry):
        step(4 * jj, None, tiles=4)
        return carry

    lax.fori_loop(0, n_far // 4, far_quad, 0)

    @pl.when(n_far % 4 >= 2)
    def _():
        step(n_far // 4 * 4, None, tiles=2)

    @pl.when(n_far % 2 == 1)
    def _():
        step(n_far - 1, None)

    @pl.when(i > 0)
    def _():
        step(i - 1, jnp.concatenate([d1_ref[0], d0_ref[0]], axis=1), tiles=2)

    @pl.when(i == 0)
    def _():
        step(0, d0_ref[0])

    outs = [acc_ref[c] / jnp.sum(l_ref[c], axis=-1, keepdims=True) for c in range(2)]
    o_ref[0, rows, :] = _sub_norm(outs[0] - lam * outs[1], g_ref[...], lam_init).astype(BF16)


def _attn_call(q, k, v, d0, d1, dl, g, lam_init):
    bsz, s, width = q.shape
    heads = width // V_DIM
    t = min(ATTN_TILE, s)
    per_step = math.gcd(ATTN_TILES_PER_STEP, s // t)
    kv_spec = pl.BlockSpec((1, s, V_DIM), lambda b, h, i: (b, 0, h))
    tile_spec = pl.BlockSpec((1, per_step * t, V_DIM), lambda b, h, i: (b, i, h))
    return pl.pallas_call(
        functools.partial(_attn_kernel, lam_init),
        out_shape=jax.ShapeDtypeStruct((bsz, s, width), BF16),
        grid=(bsz, heads, s // (per_step * t)),
        in_specs=[tile_spec, kv_spec, kv_spec,
                  pl.BlockSpec((1, t, t), lambda b, h, i: (h, 0, 0)),
                  pl.BlockSpec((1, t, t), lambda b, h, i: (h, 0, 0)),
                  pl.BlockSpec(dl.shape, lambda b, h, i: (0, 0)),
                  pl.BlockSpec(g.shape, lambda b, h, i: (0, 0))],
        out_specs=tile_spec,
        scratch_shapes=[pltpu.VMEM((2, t, LANES), F32), pltpu.VMEM((2, t, LANES), F32),
                        pltpu.VMEM((2, t, V_DIM), F32)],
        compiler_params=_params(("arbitrary", "arbitrary", "arbitrary")),
        name="attn_prompt",
    )(q, k, v, d0, d1, dl, g)


def _attn_cache_kernel(lam_init, heads, q_ref, ck_ref, cv_ref, nk_ref, nv_ref, bn_ref, bw_ref, dl_ref, g_ref,
                       o_ref):
    past = ck_ref.shape[2] // heads
    far_n = past - NEAR_WINDOW
    lam = _lambda_value(dl_ref, lam_init)
    for hd in range(heads):
        lanes = slice(hd * V_DIM, (hd + 1) * V_DIM)
        kb = ck_ref[0, 0, pl.ds(hd, past, stride=heads), :].astype(BF16)
        vb = cv_ref[0, 0, pl.ds(hd, past, stride=heads), :].astype(BF16)
        nkb = nk_ref[0, :, lanes]
        nvb = nv_ref[0, :, lanes]
        outs = []
        for qm in _split_components(q_ref[0, :, lanes]):
            s_far = _scores(qm, kb[:far_n])
            s_near = _scores(qm, kb[far_n:]) + bn_ref[hd]
            s_new = _scores(qm, nkb) + bw_ref[hd]
            m = jnp.maximum(jnp.max(s_far, axis=-1, keepdims=True),
                            jnp.maximum(jnp.max(s_near, axis=-1, keepdims=True),
                                        jnp.max(s_new, axis=-1, keepdims=True)))
            p_far = jnp.exp2(s_far - m)
            p_near = jnp.exp2(s_near - m)
            p_new = jnp.exp2(s_new - m)
            denom = (jnp.sum(p_far, axis=-1, keepdims=True) + jnp.sum(p_near, axis=-1, keepdims=True)
                     + jnp.sum(p_new, axis=-1, keepdims=True))
            acc = jnp.dot(p_far.astype(BF16), vb[:far_n], preferred_element_type=F32)
            acc = acc + jnp.dot(p_near.astype(BF16), vb[far_n:], preferred_element_type=F32)
            acc = acc + jnp.dot(p_new.astype(BF16), nvb, preferred_element_type=F32)
            outs.append(acc / denom)
        o = outs[0] - lam * outs[1]
        o_ref[0, :, lanes] = _sub_norm(o, g_ref[...], lam_init).astype(BF16)


def _attn_cache_call(q, cache_k, cache_v, k_new, v_new, b_near, b_new, dl, g, layer, lam_init):
    bsz, s, width = q.shape
    heads = width // V_DIM
    cache_spec = pl.BlockSpec((1, 1) + cache_k.shape[2:], lambda b: (layer, b, 0, 0))
    tile_spec = pl.BlockSpec((1, s, width), lambda b: (b, 0, 0))
    return pl.pallas_call(
        functools.partial(_attn_cache_kernel, lam_init, heads),
        out_shape=jax.ShapeDtypeStruct((bsz, s, width), BF16),
        grid=(bsz,),
        in_specs=[tile_spec, cache_spec, cache_spec, tile_spec, tile_spec,
                  _const_spec(b_near.shape), _const_spec(b_new.shape), _const_spec(dl.shape), _const_spec(g.shape)],
        out_specs=tile_spec,
        compiler_params=_params(("arbitrary",)),
        name="attn_sample",
    )(q, cache_k, cache_v, k_new, v_new, b_near, b_new, dl, g)


def _block_diag(w):
    groups, c, _ = w.shape
    eye = jnp.eye(groups, dtype=w.dtype)
    return (eye[:, None, :, None] * w[:, :, None, :]).reshape(groups * c, groups * c)


def _pad_front(state, rows):
    return jnp.pad(state, ((0, 0), (rows - state.shape[1], 0), (0, 0)))


def _layer(x, lw, alpha, pool_state, conv_state, k_all, v_all, pos0, attend):
    bsz, s, d = x.shape
    x1 = _ffn_call(x.reshape(bsz * s, d), lw["wi1"], lw["wo1"], lw["g0"], lw["b0"], alpha)
    rows = min(TOKEN_TILE, s)
    q, kb, vb, k_all, v_all, pc, ptail, ctail = _inproj_call(
        x1.reshape(bsz, s, d), lw["w_in"], pool_state, conv_state, lw["pool_w"], lw["pool_scale"], lw["conv_w"],
        k_all, v_all, pos0, rows)
    attn = attend(q, kb, vb)
    x3 = _mixffn_call(x1, pc.reshape(bsz * s, -1), attn.reshape(bsz * s, -1), lw["w_out"], lw["g1"], lw["b1"],
                      lw["wi2"], lw["wo2"], lw["g2"], lw["b2"], alpha)
    return x3.reshape(bsz, s, d), k_all, v_all, ptail, ctail


def kernel(x_prompt, x_sample, cache_k, cache_v, state_pool, state_conv, ln_g, ln_b, w_ffn_in, w_ffn_out,
           w_in, w_out, pool_w, pool_scale, conv_w, diff_lambda, subln_g, rel_bias):
    depth = w_in.shape[0]
    bp, sp, d = x_prompt.shape
    bs, ss, _ = x_sample.shape
    past = cache_k.shape[2]
    heads = cache_k.shape[3]
    alpha = (2 * depth) ** 0.25
    t = min(ATTN_TILE, sp)

    d0 = _bias_call(rel_bias, 0, 0, t, t)
    d1 = _bias_call(rel_bias, t, 0, t, t)
    b_near = _bias_call(rel_bias, past, past - NEAR_WINDOW, ss, NEAR_WINDOW)
    b_new = _bias_call(rel_bias, past, past, ss, ss)

    ck = cache_k.reshape(depth, bs, past * heads, V_DIM)
    cv = cache_v.reshape(depth, bs, past * heads, V_DIM)
    zeros_pool = jnp.zeros((bp, POOL_HIST, state_pool.shape[-1]), F32)
    zeros_conv = jnp.zeros((bp, CONV_HIST, state_conv.shape[-1]), F32)

    hp, hs = x_prompt, x_sample
    kp = vp = ks = vs = None
    tails = {"pp": [], "cp": [], "ps": [], "cs": []}
    for l in range(depth):
        lam_init = 0.8 - 0.6 * math.exp(-0.3 * l)
        lw = {
            "wi1": w_ffn_in[l, 0].astype(BF16), "wo1": w_ffn_out[l, 0].astype(BF16),
            "wi2": w_ffn_in[l, 1].astype(BF16), "wo2": w_ffn_out[l, 1].astype(BF16),
            "w_in": w_in[l].astype(BF16), "w_out": w_out[l].astype(BF16),
            "pool_w": _block_diag(pool_w[l]).astype(BF16),
            "pool_scale": pool_scale[l][None, :], "conv_w": conv_w[l],
            "g0": ln_g[l, 0][None, :], "b0": ln_b[l, 0][None, :],
            "g1": ln_g[l, 1][None, :], "b1": ln_b[l, 1][None, :],
            "g2": ln_g[l, 2][None, :], "b2": ln_b[l, 2][None, :],
        }
        dl = diff_lambda[l]
        g = subln_g[l][None, :]

        def attend_prompt(q, kb, vb, lam_init=lam_init, dl=dl, g=g):
            return _attn_call(q, kb, vb, d0, d1, dl, g, lam_init)

        def attend_sample(q, kb, vb, l=l, lam_init=lam_init, dl=dl, g=g):
            return _attn_cache_call(q, ck, cv, kb, vb, b_near, b_new, dl, g, l, lam_init)

        hp, kp, vp, pt, ct = _layer(hp, lw, alpha, zeros_pool, zeros_conv, kp, vp, 0, attend_prompt)
        tails["pp"].append(pt)
        tails["cp"].append(ct)
        hs, ks, vs, pt, ct = _layer(hs, lw, alpha, _pad_front(state_pool[l], POOL_HIST),
                                    _pad_front(state_conv[l], CONV_HIST), ks, vs, past, attend_sample)
        tails["ps"].append(pt)
        tails["cs"].append(ct)

    def states(parts, keep):
        return jnp.stack(parts)[:, :, -keep:, :]

    return (hp, hs,
            kp.reshape(depth, bp, sp, heads, V_DIM), vp.reshape(depth, bp, sp, heads, V_DIM),
            states(tails["pp"], POOL_STATE), states(tails["cp"], CONV_K - 1),
            ks.reshape(depth, bs, ss, heads, V_DIM), vs.reshape(depth, bs, ss, heads, V_DIM),
            states(tails["ps"], POOL_STATE), states(tails["cs"], CONV_K - 1))
```

```python
import functools
import math

import jax
import jax.numpy as jnp
from jax import lax
from jax.experimental import pallas as pl
from jax.experimental.pallas import tpu as pltpu

F32 = jnp.float32
BF16 = jnp.bfloat16

CHUNK = 64
POOL_WINDOWS = (2, 4, 8, 16)
POOL_STATE = max(POOL_WINDOWS) - 1
CONV_K = 3
QK_DIM = 64
V_DIM = 2 * QK_DIM
NUM_BUCKETS = 32
MAX_DISTANCE = 128
LN_EPS = 1e-5
RMS_EPS = 1e-5
NEG_INF = -1e30
LOG2E = math.log2(math.e)

LANES = 128
SUBLANES = 8
MXU_DIM = 256
VMEM_LIMIT = 56 * 1024 * 1024

TOKEN_TILE = 1024
ATTN_TILE = 512
ATTN_TILES_PER_STEP = 4
FF_CHUNK = MXU_DIM
POOL_HIST = 16
CONV_HIST = 8
NEAR_WINDOW = 256


def _const_spec(shape):
    nd = len(shape)
    return pl.BlockSpec(shape, lambda *_: (0,) * nd, pipeline_mode=pl.Buffered(1))


def _params(sem):
    return pltpu.CompilerParams(dimension_semantics=sem, vmem_limit_bytes=VMEM_LIMIT)


def _lane_blocks(s):
    return [s[:, n * LANES:(n + 1) * LANES] for n in range(s.shape[1] // LANES)]


def _tree(op, xs):
    while len(xs) > 1:
        xs = [op(xs[n], xs[n + 1]) if n + 1 < len(xs) else xs[n] for n in range(0, len(xs), 2)]
    return xs[0]


def _layer_norm(z, g, b):
    mu = jnp.mean(z, axis=-1, keepdims=True)
    zc = z - mu
    var = jnp.mean(zc * zc, axis=-1, keepdims=True)
    return zc * lax.rsqrt(var + LN_EPS) * g + b


def _swiglu(x, wi_ref, wo_ref, h_ref):
    d_ff = wo_ref.shape[0]
    xb = x.astype(BF16)
    for c in range(d_ff // FF_CHUNK):
        lo = c * FF_CHUNK
        gate = jnp.dot(xb, wi_ref[:, lo:lo + FF_CHUNK], preferred_element_type=F32)
        up = jnp.dot(xb, wi_ref[:, d_ff + lo:d_ff + lo + FF_CHUNK], preferred_element_type=F32)
        h_ref[:, lo:lo + FF_CHUNK] = (gate * jax.nn.sigmoid(gate) * up).astype(BF16)
    return jnp.dot(h_ref[...], wo_ref[...], preferred_element_type=F32)


def _ffn_kernel(alpha, x_ref, wi_ref, wo_ref, g_ref, b_ref, o_ref, h_ref):
    x = x_ref[...]
    y = _swiglu(x, wi_ref, wo_ref, h_ref)
    o_ref[...] = _layer_norm(alpha * x + 0.5 * y, g_ref[...], b_ref[...])


def _mixffn_kernel(alpha, x_ref, pc_ref, at_ref, wm_ref, g1_ref, b1_ref,
                   wi_ref, wo_ref, g2_ref, b2_ref, o_ref, h_ref):
    n_pc = pc_ref.shape[1]
    mix = jnp.dot(pc_ref[...], wm_ref[0:n_pc, :], preferred_element_type=F32)
    mix = mix + jnp.dot(at_ref[...], wm_ref[n_pc:, :], preferred_element_type=F32)
    x = _layer_norm(alpha * x_ref[...] + mix, g1_ref[...], b1_ref[...])
    y = _swiglu(x, wi_ref, wo_ref, h_ref)
    o_ref[...] = _layer_norm(alpha * x + 0.5 * y, g2_ref[...], b2_ref[...])


def _ffn_call(x, wi, wo, g, b, alpha):
    m, d = x.shape
    tm = min(TOKEN_TILE, m)
    d_ff = wo.shape[0]
    row = pl.BlockSpec((tm, d), lambda i: (i, 0))
    return pl.pallas_call(
        functools.partial(_ffn_kernel, alpha),
        out_shape=jax.ShapeDtypeStruct((m, d), F32),
        grid=(m // tm,),
        in_specs=[row, _const_spec(wi.shape), _const_spec(wo.shape),
                  _const_spec(g.shape), _const_spec(b.shape)],
        out_specs=row,
        scratch_shapes=[pltpu.VMEM((tm, d_ff), BF16)],
        compiler_params=_params(("arbitrary",)),
        name="ffn",
    )(x, wi, wo, g, b)


def _mixffn_call(x, pc, at, wm, g1, b1, wi, wo, g2, b2, alpha):
    m, d = x.shape
    tm = min(TOKEN_TILE, m)
    d_ff = wo.shape[0]
    row = pl.BlockSpec((tm, d), lambda i: (i, 0))
    half = pl.BlockSpec((tm, pc.shape[1]), lambda i: (i, 0))
    half2 = pl.BlockSpec((tm, at.shape[1]), lambda i: (i, 0))
    return pl.pallas_call(
        functools.partial(_mixffn_kernel, alpha),
        out_shape=jax.ShapeDtypeStruct((m, d), F32),
        grid=(m // tm,),
        in_specs=[row, half, half2, _const_spec(wm.shape), _const_spec(g1.shape), _const_spec(b1.shape),
                  _const_spec(wi.shape), _const_spec(wo.shape), _const_spec(g2.shape), _const_spec(b2.shape)],
        out_specs=row,
        scratch_shapes=[pltpu.VMEM((tm, d_ff), BF16)],
        compiler_params=_params(("arbitrary",)),
        name="mixffn",
    )(x, pc, at, wm, g1, b1, wi, wo, g2, b2)


def _inproj_kernel(pos0, n_prev, x_ref, w_ref, pst_ref, cst_ref, pw_ref, ps_ref, cw_ref, *refs):
    prev_refs, refs = refs[:2 if n_prev else 0], refs[2 if n_prev else 0:]
    q_ref, kb_ref, vb_ref, k_ref, v_ref, pc_ref, pt_ref, ct_ref, pext_ref, cext_ref = refs
    nb, rows, d = x_ref.shape
    pool_w = pst_ref.shape[2]
    conv_w = cst_ref.shape[2]
    attn_w = q_ref.shape[2]
    heads = attn_w // V_DIM
    i = pl.program_id(1)

    @pl.when(i == 0)
    def _():
        pext_ref[:, 0:POOL_HIST, :] = pst_ref[...]
        cext_ref[:, 0:CONV_HIST, :] = cst_ref[...]

    @pl.when(i > 0)
    def _():
        pext_ref[:, 0:POOL_HIST, :] = pext_ref[:, rows:rows + POOL_HIST, :]
        cext_ref[:, 0:CONV_HIST, :] = cext_ref[:, rows:rows + CONV_HIST, :]

    xb = x_ref[...].reshape(nb * rows, d).astype(BF16)

    def proj(lo, n):
        return jnp.dot(xb, w_ref[:, lo:lo + n], preferred_element_type=F32)

    off = 0
    u = proj(off, pool_w); off += pool_w
    b_gate = proj(off, conv_w); off += conv_w
    c_gate = proj(off, conv_w); off += conv_w
    h_conv = proj(off, conv_w); off += conv_w
    pext_ref[:, POOL_HIST:, :] = u.reshape(nb, rows, pool_w)
    cext_ref[:, CONV_HIST:, :] = (c_gate * h_conv).reshape(nb, rows, conv_w)

    group = pool_w // len(POOL_WINDOWS)
    lane = lax.broadcasted_iota(jnp.int32, (1, pool_w), 1)
    pos = pos0 + i * rows + lax.broadcasted_iota(jnp.int32, (rows, 1), 0)
    cnt = [jnp.minimum(w, pos + 1).astype(F32) for w in POOL_WINDOWS]
    count = jnp.where(lane < group, cnt[0],
                      jnp.where(lane < 2 * group, cnt[1],
                                jnp.where(lane < 3 * group, cnt[2], cnt[3])))
    for n in range(nb):
        e = pext_ref[n]
        s2 = e + pltpu.roll(e, 1, 0)
        s4 = s2 + pltpu.roll(s2, 2, 0)
        s8 = s4 + pltpu.roll(s4, 4, 0)
        s16 = s8 + pltpu.roll(s8, 8, 0)
        win = jnp.where(lane < group, s2,
                        jnp.where(lane < 2 * group, s4,
                                  jnp.where(lane < 3 * group, s8, s16)))[POOL_HIST:]
        diff = (win / count - e[POOL_HIST:]).astype(BF16)
        pooled = jnp.dot(diff, pw_ref[...], preferred_element_type=F32) * ps_ref[...]
        pc_ref[n, :, 0:pool_w] = pooled.astype(BF16)

        z = cext_ref[n]
        conv = cw_ref[2:3, :] * z + cw_ref[1:2, :] * pltpu.roll(z, 1, 0) + cw_ref[0:1, :] * pltpu.roll(z, 2, 0)
        gated = b_gate[n * rows:(n + 1) * rows] * conv[CONV_HIST:]
        pc_ref[n, :, pool_w:pool_w + conv_w] = gated.astype(BF16)

    pt_ref[...] = pext_ref[:, rows:rows + POOL_HIST, :]
    ct_ref[...] = cext_ref[:, rows:rows + CONV_HIST, :]

    q = proj(off, attn_w); off += attn_w
    q_ref[...] = (q * (QK_DIM ** -0.5 * LOG2E)).astype(BF16).reshape(nb, rows, attn_w)
    for idx, (dense_ref, out_ref) in enumerate(((kb_ref, k_ref), (vb_ref, v_ref))):
        kv = proj(off, attn_w); off += attn_w
        dense_ref[...] = kv.astype(BF16).reshape(nb, rows, attn_w)
        if n_prev:
            out_ref[0:n_prev] = prev_refs[idx][...]
        for n in range(nb):
            for hd in range(heads):
                out_ref[n_prev, n, pl.ds(hd, rows, stride=heads), :] = kv[n * rows:(n + 1) * rows,
                                                                          hd * V_DIM:(hd + 1) * V_DIM]


def _inproj_call(x, w_in, pool_state, conv_state, pool_w_bd, pool_scale, conv_w, k_prev, v_prev, pos0, rows):
    bsz, s, d = x.shape
    nb = bsz if rows == s and bsz * s <= TOKEN_TILE else 1
    attn_w = (w_in.shape[1] - pool_state.shape[2] - 3 * conv_state.shape[2]) // 3
    heads = attn_w // V_DIM
    pool_w = pool_state.shape[2]
    conv_dim = conv_state.shape[2]
    n_prev = 0 if k_prev is None else k_prev.shape[0]

    def tile(width):
        return pl.BlockSpec((nb, rows, width), lambda b, i: (b, i, 0))

    def layers_spec(n):
        return pl.BlockSpec((n, nb, rows * heads, V_DIM), lambda b, i: (0, b, i, 0))

    in_specs = [tile(d), _const_spec(w_in.shape),
                pl.BlockSpec((nb, POOL_HIST, pool_w), lambda b, i: (b, 0, 0)),
                pl.BlockSpec((nb, CONV_HIST, conv_dim), lambda b, i: (b, 0, 0)),
                _const_spec(pool_w_bd.shape), _const_spec(pool_scale.shape), _const_spec(conv_w.shape)]
    args = [x, w_in, pool_state, conv_state, pool_w_bd, pool_scale, conv_w]
    if n_prev:
        in_specs += [layers_spec(n_prev), layers_spec(n_prev)]
        args += [k_prev, v_prev]
    kv_shape = jax.ShapeDtypeStruct((n_prev + 1, bsz, s * heads, V_DIM), F32)
    dense = jax.ShapeDtypeStruct((bsz, s, attn_w), BF16)
    out_shape = (dense, dense, dense, kv_shape, kv_shape,
                 jax.ShapeDtypeStruct((bsz, s, pool_w + conv_dim), BF16),
                 jax.ShapeDtypeStruct((bsz, POOL_HIST, pool_w), F32),
                 jax.ShapeDtypeStruct((bsz, CONV_HIST, conv_dim), F32))
    out_specs = (tile(attn_w), tile(attn_w), tile(attn_w), layers_spec(n_prev + 1), layers_spec(n_prev + 1),
                 tile(pool_w + conv_dim),
                 pl.BlockSpec((nb, POOL_HIST, pool_w), lambda b, i: (b, 0, 0)),
                 pl.BlockSpec((nb, CONV_HIST, conv_dim), lambda b, i: (b, 0, 0)))
    return pl.pallas_call(
        functools.partial(_inproj_kernel, pos0, n_prev),
        out_shape=out_shape,
        grid=(bsz // nb, s // rows),
        in_specs=in_specs,
        out_specs=out_specs,
        scratch_shapes=[pltpu.VMEM((nb, POOL_HIST + rows, pool_w), F32),
                        pltpu.VMEM((nb, CONV_HIST + rows, conv_dim), F32)],
        compiler_params=_params(("arbitrary", "arbitrary")),
        name="inproj",
    )(*args)


def _bias_kernel(q0, k0, ref_bucket, rb_ref, o_ref):
    h = pl.program_id(0)
    _, nr, nc = o_ref.shape
    qp = q0 + lax.broadcasted_iota(jnp.int32, (nr, nc), 0)
    kp = k0 + lax.broadcasted_iota(jnp.int32, (nr, nc), 1)
    rel = kp - qp
    n = jnp.abs(rel)
    nb = NUM_BUCKETS // 2
    max_exact = nb // 2
    steps = nb - max_exact
    large = jnp.full((nr, nc), max_exact, jnp.int32)
    for j in range(1, steps):
        thr = math.ceil(max_exact * (MAX_DISTANCE / max_exact) ** (j / steps) - 1e-9)
        large = large + (n >= thr).astype(jnp.int32)
    bucket = jnp.where(rel > 0, nb, 0) + jnp.where(n < max_exact, n, large)
    val = jnp.zeros((nr, nc), F32)
    for bkt in range(NUM_BUCKETS):
        val = jnp.where(bucket == bkt, rb_ref[bkt, h], val)
    val = (val - rb_ref[ref_bucket, h]) * LOG2E
    visible = (kp // CHUNK) <= (qp // CHUNK)
    o_ref[0] = jnp.where(visible, val, NEG_INF)


def _bias_call(rel_bias, q0, k0, nr, nc):
    heads = rel_bias.shape[1]
    ref_bucket = NUM_BUCKETS // 2 - 1
    return pl.pallas_call(
        functools.partial(_bias_kernel, q0, k0, ref_bucket),
        out_shape=jax.ShapeDtypeStruct((heads, nr, nc), F32),
        grid=(heads,),
        in_specs=[pl.BlockSpec(memory_space=pltpu.SMEM)],
        out_specs=pl.BlockSpec((1, nr, nc), lambda h: (h, 0, 0)),
        compiler_params=_params(("arbitrary",)),
        name="relbias",
    )(rel_bias)


def _lambda_value(dl_ref, lam_init):
    dl = dl_ref[...]
    a = jnp.sum(dl[0:1] * dl[1:2], axis=-1, keepdims=True)
    b = jnp.sum(dl[2:3] * dl[3:4], axis=-1, keepdims=True)
    return jnp.exp(a) - jnp.exp(b) + lam_init


def _split_components(q):
    lane = lax.broadcasted_iota(jnp.int32, q.shape, 1)
    zero = jnp.zeros_like(q)
    return jnp.where(lane < QK_DIM, q, zero), jnp.where(lane >= QK_DIM, q, zero)


def _scores(qm, kt):
    return lax.dot_general(qm, kt, (((1,), (1,)), ((), ())), preferred_element_type=F32)


def _sub_norm(o, g, lam_init):
    o = o * lax.rsqrt(jnp.mean(o * o, axis=-1, keepdims=True) + RMS_EPS) * g
    return o * (1.0 - lam_init)


def _attn_kernel(lam_init, q_ref, k_ref, v_ref, d0_ref, d1_ref, dl_ref, g_ref, o_ref, m_ref, l_ref, acc_ref):
    t = d0_ref.shape[1]
    per_step = q_ref.shape[1] // t
    lam = _lambda_value(dl_ref, lam_init)
    for sub in range(per_step):
        _attn_query_tile(pl.program_id(2) * per_step + sub, slice(sub * t, (sub + 1) * t), lam, lam_init,
                         q_ref, k_ref, v_ref, d0_ref, d1_ref, g_ref, o_ref, m_ref, l_ref, acc_ref)


def _attn_query_tile(i, rows, lam, lam_init, q_ref, k_ref, v_ref, d0_ref, d1_ref, g_ref, o_ref,
                     m_ref, l_ref, acc_ref):
    t = d0_ref.shape[1]
    qms = _split_components(q_ref[0, rows, :])
    m_ref[...] = jnp.full(m_ref.shape, NEG_INF, F32)
    l_ref[...] = jnp.zeros(l_ref.shape, F32)
    acc_ref[...] = jnp.zeros(acc_ref.shape, F32)

    def step(j, bias, tiles=1):
        off = pl.multiple_of(j * t, t)
        kt = k_ref[0, pl.ds(off, tiles * t), :]
        vt = v_ref[0, pl.ds(off, tiles * t), :]
        scores = [_scores(qm, kt) for qm in qms]
        if bias is not None:
            scores = [s + bias for s in scores]
        for c, s in enumerate(scores):
            blocks = _lane_blocks(s)
            m_prev = m_ref[c]
            m_cur = jnp.maximum(m_prev, jnp.max(_tree(jnp.maximum, blocks), axis=-1, keepdims=True))
            scale = jnp.exp2(m_prev - m_cur)
            ps = [jnp.exp2(blk - m_cur) for blk in blocks]
            l_ref[c] = scale * l_ref[c] + _tree(jnp.add, ps)
            p = jnp.concatenate([x.astype(BF16) for x in ps], axis=1)
            acc_ref[c] = scale * acc_ref[c] + jnp.dot(p, vt, preferred_element_type=F32)
            m_ref[c] = m_cur

    n_far = jnp.maximum(i - 1, 0)

    def far_quad(jj, carry):
        step(4 * jj, None, tiles=4)
        return carry

    lax.fori_loop(0, n_far // 4, far_quad, 0)

    @pl.when(n_far % 4 >= 2)
    def _():
        step(n_far // 4 * 4, None, tiles=2)

    @pl.when(n_far % 2 == 1)
    def _():
        step(n_far - 1, None)

    @pl.when(i > 0)
    def _():
        step(i - 1, jnp.concatenate([d1_ref[0], d0_ref[0]], axis=1), tiles=2)

    @pl.when(i == 0)
    def _():
        step(0, d0_ref[0])

    outs = [acc_ref[c] / jnp.sum(l_ref[c], axis=-1, keepdims=True) for c in range(2)]
    o_ref[0, rows, :] = _sub_norm(outs[0] - lam * outs[1], g_ref[...], lam_init).astype(BF16)


def _attn_call(q, k, v, d0, d1, dl, g, lam_init):
    bsz, s, width = q.shape
    heads = width // V_DIM
    t = min(ATTN_TILE, s)
    per_step = math.gcd(ATTN_TILES_PER_STEP, s // t)
    kv_spec = pl.BlockSpec((1, s, V_DIM), lambda b, h, i: (b, 0, h))
    tile_spec = pl.BlockSpec((1, per_step * t, V_DIM), lambda b, h, i: (b, i, h))
    return pl.pallas_call(
        functools.partial(_attn_kernel, lam_init),
        out_shape=jax.ShapeDtypeStruct((bsz, s, width), BF16),
        grid=(bsz, heads, s // (per_step * t)),
        in_specs=[tile_spec, kv_spec, kv_spec,
                  pl.BlockSpec((1, t, t), lambda b, h, i: (h, 0, 0)),
                  pl.BlockSpec((1, t, t), lambda b, h, i: (h, 0, 0)),
                  pl.BlockSpec(dl.shape, lambda b, h, i: (0, 0)),
                  pl.BlockSpec(g.shape, lambda b, h, i: (0, 0))],
        out_specs=tile_spec,
        scratch_shapes=[pltpu.VMEM((2, t, LANES), F32), pltpu.VMEM((2, t, LANES), F32),
                        pltpu.VMEM((2, t, V_DIM), F32)],
        compiler_params=_params(("arbitrary", "arbitrary", "arbitrary")),
        name="attn_prompt",
    )(q, k, v, d0, d1, dl, g)


def _attn_cache_kernel(lam_init, heads, q_ref, ck_ref, cv_ref, nk_ref, nv_ref, bn_ref, bw_ref, dl_ref, g_ref,
                       o_ref):
    past = ck_ref.shape[2] // heads
    far_n = past - NEAR_WINDOW
    lam = _lambda_value(dl_ref, lam_init)
    for hd in range(heads):
        lanes = slice(hd * V_DIM, (hd + 1) * V_DIM)
        kb = ck_ref[0, 0, pl.ds(hd, past, stride=heads), :].astype(BF16)
        vb = cv_ref[0, 0, pl.ds(hd, past, stride=heads), :].astype(BF16)
        nkb = nk_ref[0, :, lanes]
        nvb = nv_ref[0, :, lanes]
        qm = jnp.concatenate(_split_components(q_ref[0, :, lanes]), axis=0)
        n_q = qm.shape[0] // 2
        s_far = _scores(qm, kb[:far_n])
        s_near = _scores(qm, kb[far_n:]) + jnp.concatenate([bn_ref[hd]] * 2, axis=0)
        s_new = _scores(qm, nkb) + jnp.concatenate([bw_ref[hd]] * 2, axis=0)
        m = jnp.maximum(jnp.max(s_far, axis=-1, keepdims=True),
                        jnp.maximum(jnp.max(s_near, axis=-1, keepdims=True),
                                    jnp.max(s_new, axis=-1, keepdims=True)))
        p_far = jnp.exp2(s_far - m)
        p_near = jnp.exp2(s_near - m)
        p_new = jnp.exp2(s_new - m)
        denom = (jnp.sum(p_far, axis=-1, keepdims=True) + jnp.sum(p_near, axis=-1, keepdims=True)
                 + jnp.sum(p_new, axis=-1, keepdims=True))
        acc = jnp.dot(p_far.astype(BF16), vb[:far_n], preferred_element_type=F32)
        acc = acc + jnp.dot(p_near.astype(BF16), vb[far_n:], preferred_element_type=F32)
        acc = acc + jnp.dot(p_new.astype(BF16), nvb, preferred_element_type=F32)
        out = acc / denom
        o = out[:n_q] - lam * out[n_q:]
        o_ref[0, :, lanes] = _sub_norm(o, g_ref[...], lam_init).astype(BF16)


def _attn_cache_call(q, cache_k, cache_v, k_new, v_new, b_near, b_new, dl, g, layer, lam_init):
    bsz, s, width = q.shape
    heads = width // V_DIM
    cache_spec = pl.BlockSpec((1, 1) + cache_k.shape[2:], lambda b: (layer, b, 0, 0))
    tile_spec = pl.BlockSpec((1, s, width), lambda b: (b, 0, 0))
    return pl.pallas_call(
        functools.partial(_attn_cache_kernel, lam_init, heads),
        out_shape=jax.ShapeDtypeStruct((bsz, s, width), BF16),
        grid=(bsz,),
        in_specs=[tile_spec, cache_spec, cache_spec, tile_spec, tile_spec,
                  _const_spec(b_near.shape), _const_spec(b_new.shape), _const_spec(dl.shape), _const_spec(g.shape)],
        out_specs=tile_spec,
        compiler_params=_params(("arbitrary",)),
        name="attn_sample",
    )(q, cache_k, cache_v, k_new, v_new, b_near, b_new, dl, g)


def _block_diag(w):
    groups, c, _ = w.shape
    eye = jnp.eye(groups, dtype=w.dtype)
    return (eye[:, None, :, None] * w[:, :, None, :]).reshape(groups * c, groups * c)


def _pad_front(state, rows):
    return jnp.pad(state, ((0, 0), (rows - state.shape[1], 0), (0, 0)))


def _layer(x, lw, alpha, pool_state, conv_state, k_all, v_all, pos0, attend):
    bsz, s, d = x.shape
    x1 = _ffn_call(x.reshape(bsz * s, d), lw["wi1"], lw["wo1"], lw["g0"], lw["b0"], alpha)
    rows = min(TOKEN_TILE, s)
    q, kb, vb, k_all, v_all, pc, ptail, ctail = _inproj_call(
        x1.reshape(bsz, s, d), lw["w_in"], pool_state, conv_state, lw["pool_w"], lw["pool_scale"], lw["conv_w"],
        k_all, v_all, pos0, rows)
    attn = attend(q, kb, vb)
    x3 = _mixffn_call(x1, pc.reshape(bsz * s, -1), attn.reshape(bsz * s, -1), lw["w_out"], lw["g1"], lw["b1"],
                      lw["wi2"], lw["wo2"], lw["g2"], lw["b2"], alpha)
    return x3.reshape(bsz, s, d), k_all, v_all, ptail, ctail


def kernel(x_prompt, x_sample, cache_k, cache_v, state_pool, state_conv, ln_g, ln_b, w_ffn_in, w_ffn_out,
           w_in, w_out, pool_w, pool_scale, conv_w, diff_lambda, subln_g, rel_bias):
    depth = w_in.shape[0]
    bp, sp, d = x_prompt.shape
    bs, ss, _ = x_sample.shape
    past = cache_k.shape[2]
    heads = cache_k.shape[3]
    alpha = (2 * depth) ** 0.25
    t = min(ATTN_TILE, sp)

    d0 = _bias_call(rel_bias, 0, 0, t, t)
    d1 = _bias_call(rel_bias, t, 0, t, t)
    b_near = _bias_call(rel_bias, past, past - NEAR_WINDOW, ss, NEAR_WINDOW)
    b_new = _bias_call(rel_bias, past, past, ss, ss)

    ck = cache_k.reshape(depth, bs, past * heads, V_DIM)
    cv = cache_v.reshape(depth, bs, past * heads, V_DIM)
    zeros_pool = jnp.zeros((bp, POOL_HIST, state_pool.shape[-1]), F32)
    zeros_conv = jnp.zeros((bp, CONV_HIST, state_conv.shape[-1]), F32)

    hp, hs = x_prompt, x_sample
    kp = vp = ks = vs = None
    tails = {"pp": [], "cp": [], "ps": [], "cs": []}
    for l in range(depth):
        lam_init = 0.8 - 0.6 * math.exp(-0.3 * l)
        lw = {
            "wi1": w_ffn_in[l, 0].astype(BF16), "wo1": w_ffn_out[l, 0].astype(BF16),
            "wi2": w_ffn_in[l, 1].astype(BF16), "wo2": w_ffn_out[l, 1].astype(BF16),
            "w_in": w_in[l].astype(BF16), "w_out": w_out[l].astype(BF16),
            "pool_w": _block_diag(pool_w[l]).astype(BF16),
            "pool_scale": pool_scale[l][None, :], "conv_w": conv_w[l],
            "g0": ln_g[l, 0][None, :], "b0": ln_b[l, 0][None, :],
            "g1": ln_g[l, 1][None, :], "b1": ln_b[l, 1][None, :],
            "g2": ln_g[l, 2][None, :], "b2": ln_b[l, 2][None, :],
        }
        dl = diff_lambda[l]
        g = subln_g[l][None, :]

        def attend_prompt(q, kb, vb, lam_init=lam_init, dl=dl, g=g):
            return _attn_call(q, kb, vb, d0, d1, dl, g, lam_init)

        def attend_sample(q, kb, vb, l=l, lam_init=lam_init, dl=dl, g=g):
            return _attn_cache_call(q, ck, cv, kb, vb, b_near, b_new, dl, g, l, lam_init)

        hp, kp, vp, pt, ct = _layer(hp, lw, alpha, zeros_pool, zeros_conv, kp, vp, 0, attend_prompt)
        tails["pp"].append(pt)
        tails["cp"].append(ct)
        hs, ks, vs, pt, ct = _layer(hs, lw, alpha, _pad_front(state_pool[l], POOL_HIST),
                                    _pad_front(state_conv[l], CONV_HIST), ks, vs, past, attend_sample)
        tails["ps"].append(pt)
        tails["cs"].append(ct)

    def states(parts, keep):
        return jnp.stack(parts)[:, :, -keep:, :]

    return (hp, hs,
            kp.reshape(depth, bp, sp, heads, V_DIM), vp.reshape(depth, bp, sp, heads, V_DIM),
            states(tails["pp"], POOL_STATE), states(tails["cp"], CONV_K - 1),
            ks.reshape(depth, bs, ss, heads, V_DIM), vs.reshape(depth, bs, ss, heads, V_DIM),
            states(tails["ps"], POOL_STATE), states(tails["cs"], CONV_K - 1))
```

```python
import functools
import math

import jax
import jax.numpy as jnp
from jax import lax
from jax.experimental import pallas as pl
from jax.experimental.pallas import tpu as pltpu

F32 = jnp.float32
BF16 = jnp.bfloat16

CHUNK = 64
POOL_WINDOWS = (2, 4, 8, 16)
POOL_STATE = max(POOL_WINDOWS) - 1
CONV_K = 3
QK_DIM = 64
V_DIM = 2 * QK_DIM
NUM_BUCKETS = 32
MAX_DISTANCE = 128
LN_EPS = 1e-5
RMS_EPS = 1e-5
NEG_INF = -1e30
LOG2E = math.log2(math.e)

LANES = 128
SUBLANES = 8
MXU_DIM = 256
VMEM_LIMIT = 56 * 1024 * 1024

TOKEN_TILE = 1024
ATTN_TILE = 512
ATTN_TILES_PER_STEP = 4
FF_CHUNK = MXU_DIM
POOL_HIST = 16
CONV_HIST = 8
NEAR_WINDOW = 256


def _const_spec(shape):
    nd = len(shape)
    return pl.BlockSpec(shape, lambda *_: (0,) * nd, pipeline_mode=pl.Buffered(1))


def _params(sem):
    return pltpu.CompilerParams(dimension_semantics=sem, vmem_limit_bytes=VMEM_LIMIT)


def _lane_blocks(s):
    return [s[:, n * LANES:(n + 1) * LANES] for n in range(s.shape[1] // LANES)]


def _tree(op, xs):
    while len(xs) > 1:
        xs = [op(xs[n], xs[n + 1]) if n + 1 < len(xs) else xs[n] for n in range(0, len(xs), 2)]
    return xs[0]


def _layer_norm(z, g, b):
    mu = jnp.mean(z, axis=-1, keepdims=True)
    zc = z - mu
    var = jnp.mean(zc * zc, axis=-1, keepdims=True)
    return zc * lax.rsqrt(var + LN_EPS) * g + b


def _swiglu(x, wi_ref, wo_ref, h_ref):
    d_ff = wo_ref.shape[0]
    xb = x.astype(BF16)
    for c in range(d_ff // FF_CHUNK):
        lo = c * FF_CHUNK
        gate = jnp.dot(xb, wi_ref[:, lo:lo + FF_CHUNK], preferred_element_type=F32)
        up = jnp.dot(xb, wi_ref[:, d_ff + lo:d_ff + lo + FF_CHUNK], preferred_element_type=F32)
        h_ref[:, lo:lo + FF_CHUNK] = (gate * jax.nn.sigmoid(gate) * up).astype(BF16)
    return jnp.dot(h_ref[...], wo_ref[...], preferred_element_type=F32)


def _ffn_kernel(alpha, x_ref, wi_ref, wo_ref, g_ref, b_ref, o_ref, h_ref):
    x = x_ref[...]
    y = _swiglu(x, wi_ref, wo_ref, h_ref)
    o_ref[...] = _layer_norm(alpha * x + 0.5 * y, g_ref[...], b_ref[...])


def _mixffn_kernel(alpha, x_ref, pc_ref, at_ref, wm_ref, g1_ref, b1_ref,
                   wi_ref, wo_ref, g2_ref, b2_ref, o_ref, h_ref):
    n_pc = pc_ref.shape[1]
    mix = jnp.dot(pc_ref[...], wm_ref[0:n_pc, :], preferred_element_type=F32)
    mix = mix + jnp.dot(at_ref[...], wm_ref[n_pc:, :], preferred_element_type=F32)
    x = _layer_norm(alpha * x_ref[...] + mix, g1_ref[...], b1_ref[...])
    y = _swiglu(x, wi_ref, wo_ref, h_ref)
    o_ref[...] = _layer_norm(alpha * x + 0.5 * y, g2_ref[...], b2_ref[...])


def _ffn_call(x, wi, wo, g, b, alpha):
    m, d = x.shape
    tm = min(TOKEN_TILE, m)
    d_ff = wo.shape[0]
    row = pl.BlockSpec((tm, d), lambda i: (i, 0))
    return pl.pallas_call(
        functools.partial(_ffn_kernel, alpha),
        out_shape=jax.ShapeDtypeStruct((m, d), F32),
        grid=(m // tm,),
        in_specs=[row, _const_spec(wi.shape), _const_spec(wo.shape),
                  _const_spec(g.shape), _const_spec(b.shape)],
        out_specs=row,
        scratch_shapes=[pltpu.VMEM((tm, d_ff), BF16)],
        compiler_params=_params(("arbitrary",)),
        name="ffn",
    )(x, wi, wo, g, b)


def _mixffn_call(x, pc, at, wm, g1, b1, wi, wo, g2, b2, alpha):
    m, d = x.shape
    tm = min(TOKEN_TILE, m)
    d_ff = wo.shape[0]
    row = pl.BlockSpec((tm, d), lambda i: (i, 0))
    half = pl.BlockSpec((tm, pc.shape[1]), lambda i: (i, 0))
    half2 = pl.BlockSpec((tm, at.shape[1]), lambda i: (i, 0))
    return pl.pallas_call(
        functools.partial(_mixffn_kernel, alpha),
        out_shape=jax.ShapeDtypeStruct((m, d), F32),
        grid=(m // tm,),
        in_specs=[row, half, half2, _const_spec(wm.shape), _const_spec(g1.shape), _const_spec(b1.shape),
                  _const_spec(wi.shape), _const_spec(wo.shape), _const_spec(g2.shape), _const_spec(b2.shape)],
        out_specs=row,
        scratch_shapes=[pltpu.VMEM((tm, d_ff), BF16)],
        compiler_params=_params(("arbitrary",)),
        name="mixffn",
    )(x, pc, at, wm, g1, b1, wi, wo, g2, b2)


def _inproj_kernel(pos0, n_prev, x_ref, w_ref, pst_ref, cst_ref, pw_ref, ps_ref, cw_ref, *refs):
    prev_refs, refs = refs[:2 if n_prev else 0], refs[2 if n_prev else 0:]
    q_ref, kb_ref, vb_ref, k_ref, v_ref, pc_ref, pt_ref, ct_ref, pext_ref, cext_ref = refs
    nb, rows, d = x_ref.shape
    pool_w = pst_ref.shape[2]
    conv_w = cst_ref.shape[2]
    attn_w = q_ref.shape[2]
    heads = attn_w // V_DIM
    i = pl.program_id(1)

    @pl.when(i == 0)
    def _():
        pext_ref[:, 0:POOL_HIST, :] = pst_ref[...]
        cext_ref[:, 0:CONV_HIST, :] = cst_ref[...]

    @pl.when(i > 0)
    def _():
        pext_ref[:, 0:POOL_HIST, :] = pext_ref[:, rows:rows + POOL_HIST, :]
        cext_ref[:, 0:CONV_HIST, :] = cext_ref[:, rows:rows + CONV_HIST, :]

    xb = x_ref[...].reshape(nb * rows, d).astype(BF16)

    def proj(lo, n):
        return jnp.dot(xb, w_ref[:, lo:lo + n], preferred_element_type=F32)

    off = 0
    u = proj(off, pool_w); off += pool_w
    b_gate = proj(off, conv_w); off += conv_w
    c_gate = proj(off, conv_w); off += conv_w
    h_conv = proj(off, conv_w); off += conv_w
    pext_ref[:, POOL_HIST:, :] = u.reshape(nb, rows, pool_w)
    cext_ref[:, CONV_HIST:, :] = (c_gate * h_conv).reshape(nb, rows, conv_w)

    group = pool_w // len(POOL_WINDOWS)
    lane = lax.broadcasted_iota(jnp.int32, (1, pool_w), 1)
    pos = pos0 + i * rows + lax.broadcasted_iota(jnp.int32, (rows, 1), 0)
    cnt = [jnp.minimum(w, pos + 1).astype(F32) for w in POOL_WINDOWS]
    count = jnp.where(lane < group, cnt[0],
                      jnp.where(lane < 2 * group, cnt[1],
                                jnp.where(lane < 3 * group, cnt[2], cnt[3])))
    for n in range(nb):
        e = pext_ref[n]
        s2 = e + pltpu.roll(e, 1, 0)
        s4 = s2 + pltpu.roll(s2, 2, 0)
        s8 = s4 + pltpu.roll(s4, 4, 0)
        s16 = s8 + pltpu.roll(s8, 8, 0)
        win = jnp.where(lane < group, s2,
                        jnp.where(lane < 2 * group, s4,
                                  jnp.where(lane < 3 * group, s8, s16)))[POOL_HIST:]
        diff = (win / count - e[POOL_HIST:]).astype(BF16)
        pooled = jnp.dot(diff, pw_ref[...], preferred_element_type=F32) * ps_ref[...]
        pc_ref[n, :, 0:pool_w] = pooled.astype(BF16)

        z = cext_ref[n]
        conv = cw_ref[2:3, :] * z + cw_ref[1:2, :] * pltpu.roll(z, 1, 0) + cw_ref[0:1, :] * pltpu.roll(z, 2, 0)
        gated = b_gate[n * rows:(n + 1) * rows] * conv[CONV_HIST:]
        pc_ref[n, :, pool_w:pool_w + conv_w] = gated.astype(BF16)

    pt_ref[...] = pext_ref[:, rows:rows + POOL_HIST, :]
    ct_ref[...] = cext_ref[:, rows:rows + CONV_HIST, :]

    q = proj(off, attn_w); off += attn_w
    q_ref[...] = (q * (QK_DIM ** -0.5 * LOG2E)).astype(BF16).reshape(nb, rows, attn_w)
    for idx, (dense_ref, out_ref) in enumerate(((kb_ref, k_ref), (vb_ref, v_ref))):
        kv = proj(off, attn_w); off += attn_w
        dense_ref[...] = kv.astype(BF16).reshape(nb, rows, attn_w)
        if n_prev:
            out_ref[0:n_prev] = prev_refs[idx][...]
        for n in range(nb):
            for hd in range(heads):
                out_ref[n_prev, n, pl.ds(hd, rows, stride=heads), :] = kv[n * rows:(n + 1) * rows,
                                                                          hd * V_DIM:(hd + 1) * V_DIM]


def _inproj_call(x, w_in, pool_state, conv_state, pool_w_bd, pool_scale, conv_w, k_prev, v_prev, pos0, rows):
    bsz, s, d = x.shape
    nb = bsz if rows == s and bsz * s <= TOKEN_TILE else 1
    attn_w = (w_in.shape[1] - pool_state.shape[2] - 3 * conv_state.shape[2]) // 3
    heads = attn_w // V_DIM
    pool_w = pool_state.shape[2]
    conv_dim = conv_state.shape[2]
    n_prev = 0 if k_prev is None else k_prev.shape[0]

    def tile(width):
        return pl.BlockSpec((nb, rows, width), lambda b, i: (b, i, 0))

    def layers_spec(n):
        return pl.BlockSpec((n, nb, rows * heads, V_DIM), lambda b, i: (0, b, i, 0))

    in_specs = [tile(d), _const_spec(w_in.shape),
                pl.BlockSpec((nb, POOL_HIST, pool_w), lambda b, i: (b, 0, 0)),
                pl.BlockSpec((nb, CONV_HIST, conv_dim), lambda b, i: (b, 0, 0)),
                _const_spec(pool_w_bd.shape), _const_spec(pool_scale.shape), _const_spec(conv_w.shape)]
    args = [x, w_in, pool_state, conv_state, pool_w_bd, pool_scale, conv_w]
    if n_prev:
        in_specs += [layers_spec(n_prev), layers_spec(n_prev)]
        args += [k_prev, v_prev]
    kv_shape = jax.ShapeDtypeStruct((n_prev + 1, bsz, s * heads, V_DIM), F32)
    dense = jax.ShapeDtypeStruct((bsz, s, attn_w), BF16)
    out_shape = (dense, dense, dense, kv_shape, kv_shape,
                 jax.ShapeDtypeStruct((bsz, s, pool_w + conv_dim), BF16),
                 jax.ShapeDtypeStruct((bsz, POOL_HIST, pool_w), F32),
                 jax.ShapeDtypeStruct((bsz, CONV_HIST, conv_dim), F32))
    out_specs = (tile(attn_w), tile(attn_w), tile(attn_w), layers_spec(n_prev + 1), layers_spec(n_prev + 1),
                 tile(pool_w + conv_dim),
                 pl.BlockSpec((nb, POOL_HIST, pool_w), lambda b, i: (b, 0, 0)),
                 pl.BlockSpec((nb, CONV_HIST, conv_dim), lambda b, i: (b, 0, 0)))
    return pl.pallas_call(
        functools.partial(_inproj_kernel, pos0, n_prev),
        out_shape=out_shape,
        grid=(bsz // nb, s // rows),
        in_specs=in_specs,
        out_specs=out_specs,
        scratch_shapes=[pltpu.VMEM((nb, POOL_HIST + rows, pool_w), F32),
                        pltpu.VMEM((nb, CONV_HIST + rows, conv_dim), F32)],
        compiler_params=_params(("arbitrary", "arbitrary")),
        name="inproj",
    )(*args)


def _bias_kernel(q0, k0, ref_bucket, rb_ref, o_ref):
    h = pl.program_id(0)
    _, nr, nc = o_ref.shape
    qp = q0 + lax.broadcasted_iota(jnp.int32, (nr, nc), 0)
    kp = k0 + lax.broadcasted_iota(jnp.int32, (nr, nc), 1)
    rel = kp - qp
    n = jnp.abs(rel)
    nb = NUM_BUCKETS // 2
    max_exact = nb // 2
    steps = nb - max_exact
    large = jnp.full((nr, nc), max_exact, jnp.int32)
    for j in range(1, steps):
        thr = math.ceil(max_exact * (MAX_DISTANCE / max_exact) ** (j / steps) - 1e-9)
        large = large + (n >= thr).astype(jnp.int32)
    bucket = jnp.where(rel > 0, nb, 0) + jnp.where(n < max_exact, n, large)
    val = jnp.zeros((nr, nc), F32)
    for bkt in range(NUM_BUCKETS):
        val = jnp.where(bucket == bkt, rb_ref[bkt, h], val)
    val = (val - rb_ref[ref_bucket, h]) * LOG2E
    visible = (kp // CHUNK) <= (qp // CHUNK)
    o_ref[0] = jnp.where(visible, val, NEG_INF)


def _bias_call(rel_bias, q0, k0, nr, nc):
    heads = rel_bias.shape[1]
    ref_bucket = NUM_BUCKETS // 2 - 1
    return pl.pallas_call(
        functools.partial(_bias_kernel, q0, k0, ref_bucket),
        out_shape=jax.ShapeDtypeStruct((heads, nr, nc), F32),
        grid=(heads,),
        in_specs=[pl.BlockSpec(memory_space=pltpu.SMEM)],
        out_specs=pl.BlockSpec((1, nr, nc), lambda h: (h, 0, 0)),
        compiler_params=_params(("arbitrary",)),
        name="relbias",
    )(rel_bias)


def _lambda_value(dl_ref, lam_init):
    dl = dl_ref[...]
    a = jnp.sum(dl[0:1] * dl[1:2], axis=-1, keepdims=True)
    b = jnp.sum(dl[2:3] * dl[3:4], axis=-1, keepdims=True)
    return jnp.exp(a) - jnp.exp(b) + lam_init


def _split_components(q):
    lane = lax.broadcasted_iota(jnp.int32, q.shape, 1)
    zero = jnp.zeros_like(q)
    return jnp.where(lane < QK_DIM, q, zero), jnp.where(lane >= QK_DIM, q, zero)


def _scores(qm, kt):
    return lax.dot_general(qm, kt, (((1,), (1,)), ((), ())), preferred_element_type=F32)


def _sub_norm(o, g, lam_init):
    o = o * lax.rsqrt(jnp.mean(o * o, axis=-1, keepdims=True) + RMS_EPS) * g
    return o * (1.0 - lam_init)


def _attn_kernel(lam_init, q_ref, k_ref, v_ref, d0_ref, d1_ref, dl_ref, g_ref, o_ref, m_ref, l_ref, acc_ref):
    t = d0_ref.shape[1]
    per_step = q_ref.shape[1] // t
    lam = _lambda_value(dl_ref, lam_init)
    for sub in range(per_step):
        _attn_query_tile(pl.program_id(2) * per_step + sub, slice(sub * t, (sub + 1) * t), lam, lam_init,
                         q_ref, k_ref, v_ref, d0_ref, d1_ref, g_ref, o_ref, m_ref, l_ref, acc_ref)


def _attn_query_tile(i, rows, lam, lam_init, q_ref, k_ref, v_ref, d0_ref, d1_ref, g_ref, o_ref,
                     m_ref, l_ref, acc_ref):
    t = d0_ref.shape[1]
    qm = jnp.concatenate(_split_components(q_ref[0, rows, :]), axis=0)
    m_ref[...] = jnp.full(m_ref.shape, NEG_INF, F32)
    l_ref[...] = jnp.zeros(l_ref.shape, F32)
    acc_ref[...] = jnp.zeros(acc_ref.shape, F32)

    def step(j, bias, tiles=1):
        off = pl.multiple_of(j * t, t)
        kt = k_ref[0, pl.ds(off, tiles * t), :]
        vt = v_ref[0, pl.ds(off, tiles * t), :]
        s = _scores(qm, kt)
        if bias is not None:
            s = s + jnp.concatenate([bias, bias], axis=0)
        blocks = _lane_blocks(s)
        m_prev = m_ref[...]
        m_cur = jnp.maximum(m_prev, jnp.max(_tree(jnp.maximum, blocks), axis=-1, keepdims=True))
        scale = jnp.exp2(m_prev - m_cur)
        ps = [jnp.exp2(blk - m_cur) for blk in blocks]
        l_ref[...] = scale * l_ref[...] + _tree(jnp.add, ps)
        p = jnp.concatenate([x.astype(BF16) for x in ps], axis=1)
        acc_ref[...] = scale * acc_ref[...] + jnp.dot(p, vt, preferred_element_type=F32)
        m_ref[...] = m_cur

    n_far = jnp.maximum(i - 1, 0)

    def far_quad(jj, carry):
        step(4 * jj, None, tiles=4)
        return carry

    lax.fori_loop(0, n_far // 4, far_quad, 0)

    @pl.when(n_far % 4 >= 2)
    def _():
        step(n_far // 4 * 4, None, tiles=2)

    @pl.when(n_far % 2 == 1)
    def _():
        step(n_far - 1, None)

    @pl.when(i > 0)
    def _():
        step(i - 1, jnp.concatenate([d1_ref[0], d0_ref[0]], axis=1), tiles=2)

    @pl.when(i == 0)
    def _():
        step(0, d0_ref[0])

    out = acc_ref[...] / jnp.sum(l_ref[...], axis=-1, keepdims=True)
    o_ref[0, rows, :] = _sub_norm(out[:t] - lam * out[t:], g_ref[...], lam_init).astype(BF16)


def _attn_call(q, k, v, d0, d1, dl, g, lam_init):
    bsz, s, width = q.shape
    heads = width // V_DIM
    t = min(ATTN_TILE, s)
    per_step = math.gcd(ATTN_TILES_PER_STEP, s // t)
    kv_spec = pl.BlockSpec((1, s, V_DIM), lambda b, h, i: (b, 0, h))
    tile_spec = pl.BlockSpec((1, per_step * t, V_DIM), lambda b, h, i: (b, i, h))
    return pl.pallas_call(
        functools.partial(_attn_kernel, lam_init),
        out_shape=jax.ShapeDtypeStruct((bsz, s, width), BF16),
        grid=(bsz, heads, s // (per_step * t)),
        in_specs=[tile_spec, kv_spec, kv_spec,
                  pl.BlockSpec((1, t, t), lambda b, h, i: (h, 0, 0)),
                  pl.BlockSpec((1, t, t), lambda b, h, i: (h, 0, 0)),
                  pl.BlockSpec(dl.shape, lambda b, h, i: (0, 0)),
                  pl.BlockSpec(g.shape, lambda b, h, i: (0, 0))],
        out_specs=tile_spec,
        scratch_shapes=[pltpu.VMEM((2 * t, LANES), F32), pltpu.VMEM((2 * t, LANES), F32),
                        pltpu.VMEM((2 * t, V_DIM), F32)],
        compiler_params=_params(("arbitrary", "arbitrary", "arbitrary")),
        name="attn_prompt",
    )(q, k, v, d0, d1, dl, g)


def _attn_cache_kernel(lam_init, heads, q_ref, ck_ref, cv_ref, nk_ref, nv_ref, bn_ref, bw_ref, dl_ref, g_ref,
                       o_ref):
    past = ck_ref.shape[2] // heads
    far_n = past - NEAR_WINDOW
    lam = _lambda_value(dl_ref, lam_init)
    for hd in range(heads):
        lanes = slice(hd * V_DIM, (hd + 1) * V_DIM)
        kb = ck_ref[0, 0, pl.ds(hd, past, stride=heads), :].astype(BF16)
        vb = cv_ref[0, 0, pl.ds(hd, past, stride=heads), :].astype(BF16)
        nkb = nk_ref[0, :, lanes]
        nvb = nv_ref[0, :, lanes]
        qm = jnp.concatenate(_split_components(q_ref[0, :, lanes]), axis=0)
        n_q = qm.shape[0] // 2
        s_far = _scores(qm, kb[:far_n])
        s_near = _scores(qm, kb[far_n:]) + jnp.concatenate([bn_ref[hd]] * 2, axis=0)
        s_new = _scores(qm, nkb) + jnp.concatenate([bw_ref[hd]] * 2, axis=0)
        m = jnp.maximum(jnp.max(s_far, axis=-1, keepdims=True),
                        jnp.maximum(jnp.max(s_near, axis=-1, keepdims=True),
                                    jnp.max(s_new, axis=-1, keepdims=True)))
        p_far = jnp.exp2(s_far - m)
        p_near = jnp.exp2(s_near - m)
        p_new = jnp.exp2(s_new - m)
        denom = (jnp.sum(p_far, axis=-1, keepdims=True) + jnp.sum(p_near, axis=-1, keepdims=True)
                 + jnp.sum(p_new, axis=-1, keepdims=True))
        acc = jnp.dot(p_far.astype(BF16), vb[:far_n], preferred_element_type=F32)
        acc = acc + jnp.dot(p_near.astype(BF16), vb[far_n:], preferred_element_type=F32)
        acc = acc + jnp.dot(p_new.astype(BF16), nvb, preferred_element_type=F32)
        out = acc / denom
        o = out[:n_q] - lam * out[n_q:]
        o_ref[0, :, lanes] = _sub_norm(o, g_ref[...], lam_init).astype(BF16)


def _attn_cache_call(q, cache_k, cache_v, k_new, v_new, b_near, b_new, dl, g, layer, lam_init):
    bsz, s, width = q.shape
    heads = width // V_DIM
    cache_spec = pl.BlockSpec((1, 1) + cache_k.shape[2:], lambda b: (layer, b, 0, 0))
    tile_spec = pl.BlockSpec((1, s, width), lambda b: (b, 0, 0))
    return pl.pallas_call(
        functools.partial(_attn_cache_kernel, lam_init, heads),
        out_shape=jax.ShapeDtypeStruct((bsz, s, width), BF16),
        grid=(bsz,),
        in_specs=[tile_spec, cache_spec, cache_spec, tile_spec, tile_spec,
                  _const_spec(b_near.shape), _const_spec(b_new.shape), _const_spec(dl.shape), _const_spec(g.shape)],
        out_specs=tile_spec,
        compiler_params=_params(("arbitrary",)),
        name="attn_sample",
    )(q, cache_k, cache_v, k_new, v_new, b_near, b_new, dl, g)


def _block_diag(w):
    groups, c, _ = w.shape
    eye = jnp.eye(groups, dtype=w.dtype)
    return (eye[:, None, :, None] * w[:, :, None, :]).reshape(groups * c, groups * c)


def _pad_front(state, rows):
    return jnp.pad(state, ((0, 0), (rows - state.shape[1], 0), (0, 0)))


def _layer(x, lw, alpha, pool_state, conv_state, k_all, v_all, pos0, attend):
    bsz, s, d = x.shape
    x1 = _ffn_call(x.reshape(bsz * s, d), lw["wi1"], lw["wo1"], lw["g0"], lw["b0"], alpha)
    rows = min(TOKEN_TILE, s)
    q, kb, vb, k_all, v_all, pc, ptail, ctail = _inproj_call(
        x1.reshape(bsz, s, d), lw["w_in"], pool_state, conv_state, lw["pool_w"], lw["pool_scale"], lw["conv_w"],
        k_all, v_all, pos0, rows)
    attn = attend(q, kb, vb)
    x3 = _mixffn_call(x1, pc.reshape(bsz * s, -1), attn.reshape(bsz * s, -1), lw["w_out"], lw["g1"], lw["b1"],
                      lw["wi2"], lw["wo2"], lw["g2"], lw["b2"], alpha)
    return x3.reshape(bsz, s, d), k_all, v_all, ptail, ctail


def kernel(x_prompt, x_sample, cache_k, cache_v, state_pool, state_conv, ln_g, ln_b, w_ffn_in, w_ffn_out,
           w_in, w_out, pool_w, pool_scale, conv_w, diff_lambda, subln_g, rel_bias):
    depth = w_in.shape[0]
    bp, sp, d = x_prompt.shape
    bs, ss, _ = x_sample.shape
    past = cache_k.shape[2]
    heads = cache_k.shape[3]
    alpha = (2 * depth) ** 0.25
    t = min(ATTN_TILE, sp)

    d0 = _bias_call(rel_bias, 0, 0, t, t)
    d1 = _bias_call(rel_bias, t, 0, t, t)
    b_near = _bias_call(rel_bias, past, past - NEAR_WINDOW, ss, NEAR_WINDOW)
    b_new = _bias_call(rel_bias, past, past, ss, ss)

    ck = cache_k.reshape(depth, bs, past * heads, V_DIM)
    cv = cache_v.reshape(depth, bs, past * heads, V_DIM)
    zeros_pool = jnp.zeros((bp, POOL_HIST, state_pool.shape[-1]), F32)
    zeros_conv = jnp.zeros((bp, CONV_HIST, state_conv.shape[-1]), F32)

    hp, hs = x_prompt, x_sample
    kp = vp = ks = vs = None
    tails = {"pp": [], "cp": [], "ps": [], "cs": []}
    for l in range(depth):
        lam_init = 0.8 - 0.6 * math.exp(-0.3 * l)
        lw = {
            "wi1": w_ffn_in[l, 0].astype(BF16), "wo1": w_ffn_out[l, 0].astype(BF16),
            "wi2": w_ffn_in[l, 1].astype(BF16), "wo2": w_ffn_out[l, 1].astype(BF16),
            "w_in": w_in[l].astype(BF16), "w_out": w_out[l].astype(BF16),
            "pool_w": _block_diag(pool_w[l]).astype(BF16),
            "pool_scale": pool_scale[l][None, :], "conv_w": conv_w[l],
            "g0": ln_g[l, 0][None, :], "b0": ln_b[l, 0][None, :],
            "g1": ln_g[l, 1][None, :], "b1": ln_b[l, 1][None, :],
            "g2": ln_g[l, 2][None, :], "b2": ln_b[l, 2][None, :],
        }
        dl = diff_lambda[l]
        g = subln_g[l][None, :]

        def attend_prompt(q, kb, vb, lam_init=lam_init, dl=dl, g=g):
            return _attn_call(q, kb, vb, d0, d1, dl, g, lam_init)

        def attend_sample(q, kb, vb, l=l, lam_init=lam_init, dl=dl, g=g):
            return _attn_cache_call(q, ck, cv, kb, vb, b_near, b_new, dl, g, l, lam_init)

        hp, kp, vp, pt, ct = _layer(hp, lw, alpha, zeros_pool, zeros_conv, kp, vp, 0, attend_prompt)
        tails["pp"].append(pt)
        tails["cp"].append(ct)
        hs, ks, vs, pt, ct = _layer(hs, lw, alpha, _pad_front(state_pool[l], POOL_HIST),
                                    _pad_front(state_conv[l], CONV_HIST), ks, vs, past, attend_sample)
        tails["ps"].append(pt)
        tails["cs"].append(ct)

    def states(parts, keep):
        return jnp.stack(parts)[:, :, -keep:, :]

    return (hp, hs,
            kp.reshape(depth, bp, sp, heads, V_DIM), vp.reshape(depth, bp, sp, heads, V_DIM),
            states(tails["pp"], POOL_STATE), states(tails["cp"], CONV_K - 1),
            ks.reshape(depth, bs, ss, heads, V_DIM), vs.reshape(depth, bs, ss, heads, V_DIM),
            states(tails["ps"], POOL_STATE), states(tails["cs"], CONV_K - 1))
```

```python
import functools
import math

import jax
import jax.numpy as jnp
from jax import lax
from jax.experimental import pallas as pl
from jax.experimental.pallas import tpu as pltpu

F32 = jnp.float32
BF16 = jnp.bfloat16

CHUNK = 64
POOL_WINDOWS = (2, 4, 8, 16)
POOL_STATE = max(POOL_WINDOWS) - 1
CONV_K = 3
QK_DIM = 64
V_DIM = 2 * QK_DIM
NUM_BUCKETS = 32
MAX_DISTANCE = 128
LN_EPS = 1e-5
RMS_EPS = 1e-5
NEG_INF = -1e30
LOG2E = math.log2(math.e)

LANES = 128
SUBLANES = 8
MXU_DIM = 256
VMEM_LIMIT = 56 * 1024 * 1024

TOKEN_TILE = 1024
ATTN_TILE = 512
ATTN_TILES_PER_STEP = 8
FF_CHUNK = MXU_DIM
POOL_HIST = 16
CONV_HIST = 8
NEAR_WINDOW = 256


def _const_spec(shape):
    nd = len(shape)
    return pl.BlockSpec(shape, lambda *_: (0,) * nd, pipeline_mode=pl.Buffered(1))


def _params(sem):
    return pltpu.CompilerParams(dimension_semantics=sem, vmem_limit_bytes=VMEM_LIMIT)


def _lane_blocks(s):
    return [s[:, n * LANES:(n + 1) * LANES] for n in range(s.shape[1] // LANES)]


def _tree(op, xs):
    while len(xs) > 1:
        xs = [op(xs[n], xs[n + 1]) if n + 1 < len(xs) else xs[n] for n in range(0, len(xs), 2)]
    return xs[0]


def _layer_norm(z, g, b):
    mu = jnp.mean(z, axis=-1, keepdims=True)
    zc = z - mu
    var = jnp.mean(zc * zc, axis=-1, keepdims=True)
    return zc * lax.rsqrt(var + LN_EPS) * g + b


def _swiglu(x, wi_ref, wo_ref, h_ref):
    d_ff = wo_ref.shape[0]
    xb = x.astype(BF16)
    for c in range(d_ff // FF_CHUNK):
        lo = c * FF_CHUNK
        gate = jnp.dot(xb, wi_ref[:, lo:lo + FF_CHUNK], preferred_element_type=F32)
        up = jnp.dot(xb, wi_ref[:, d_ff + lo:d_ff + lo + FF_CHUNK], preferred_element_type=F32)
        h_ref[:, lo:lo + FF_CHUNK] = (gate * jax.nn.sigmoid(gate) * up).astype(BF16)
    return jnp.dot(h_ref[...], wo_ref[...], preferred_element_type=F32)


def _ffn_kernel(alpha, x_ref, wi_ref, wo_ref, g_ref, b_ref, o_ref, h_ref):
    x = x_ref[...]
    y = _swiglu(x, wi_ref, wo_ref, h_ref)
    o_ref[...] = _layer_norm(alpha * x + 0.5 * y, g_ref[...], b_ref[...])


def _mixffn_kernel(alpha, x_ref, pc_ref, at_ref, wm_ref, g1_ref, b1_ref,
                   wi_ref, wo_ref, g2_ref, b2_ref, o_ref, h_ref):
    n_pc = pc_ref.shape[1]
    mix = jnp.dot(pc_ref[...], wm_ref[0:n_pc, :], preferred_element_type=F32)
    mix = mix + jnp.dot(at_ref[...], wm_ref[n_pc:, :], preferred_element_type=F32)
    x = _layer_norm(alpha * x_ref[...] + mix, g1_ref[...], b1_ref[...])
    y = _swiglu(x, wi_ref, wo_ref, h_ref)
    o_ref[...] = _layer_norm(alpha * x + 0.5 * y, g2_ref[...], b2_ref[...])


def _ffn_call(x, wi, wo, g, b, alpha):
    m, d = x.shape
    tm = min(TOKEN_TILE, m)
    d_ff = wo.shape[0]
    row = pl.BlockSpec((tm, d), lambda i: (i, 0))
    return pl.pallas_call(
        functools.partial(_ffn_kernel, alpha),
        out_shape=jax.ShapeDtypeStruct((m, d), F32),
        grid=(m // tm,),
        in_specs=[row, _const_spec(wi.shape), _const_spec(wo.shape),
                  _const_spec(g.shape), _const_spec(b.shape)],
        out_specs=row,
        scratch_shapes=[pltpu.VMEM((tm, d_ff), BF16)],
        compiler_params=_params(("arbitrary",)),
        name="ffn",
    )(x, wi, wo, g, b)


def _mixffn_call(x, pc, at, wm, g1, b1, wi, wo, g2, b2, alpha):
    m, d = x.shape
    tm = min(TOKEN_TILE, m)
    d_ff = wo.shape[0]
    row = pl.BlockSpec((tm, d), lambda i: (i, 0))
    half = pl.BlockSpec((tm, pc.shape[1]), lambda i: (i, 0))
    half2 = pl.BlockSpec((tm, at.shape[1]), lambda i: (i, 0))
    return pl.pallas_call(
        functools.partial(_mixffn_kernel, alpha),
        out_shape=jax.ShapeDtypeStruct((m, d), F32),
        grid=(m // tm,),
        in_specs=[row, half, half2, _const_spec(wm.shape), _const_spec(g1.shape), _const_spec(b1.shape),
                  _const_spec(wi.shape), _const_spec(wo.shape), _const_spec(g2.shape), _const_spec(b2.shape)],
        out_specs=row,
        scratch_shapes=[pltpu.VMEM((tm, d_ff), BF16)],
        compiler_params=_params(("arbitrary",)),
        name="mixffn",
    )(x, pc, at, wm, g1, b1, wi, wo, g2, b2)


def _inproj_kernel(pos0, n_prev, x_ref, w_ref, pst_ref, cst_ref, pw_ref, ps_ref, cw_ref, *refs):
    prev_refs, refs = refs[:2 if n_prev else 0], refs[2 if n_prev else 0:]
    q_ref, kb_ref, vb_ref, k_ref, v_ref, pc_ref, pt_ref, ct_ref, pext_ref, cext_ref = refs
    nb, rows, d = x_ref.shape
    pool_w = pst_ref.shape[2]
    conv_w = cst_ref.shape[2]
    attn_w = q_ref.shape[2]
    heads = attn_w // V_DIM
    i = pl.program_id(1)

    @pl.when(i == 0)
    def _():
        pext_ref[:, 0:POOL_HIST, :] = pst_ref[...]
        cext_ref[:, 0:CONV_HIST, :] = cst_ref[...]

    @pl.when(i > 0)
    def _():
        pext_ref[:, 0:POOL_HIST, :] = pext_ref[:, rows:rows + POOL_HIST, :]
        cext_ref[:, 0:CONV_HIST, :] = cext_ref[:, rows:rows + CONV_HIST, :]

    xb = x_ref[...].reshape(nb * rows, d).astype(BF16)

    def proj(lo, n):
        return jnp.dot(xb, w_ref[:, lo:lo + n], preferred_element_type=F32)

    off = 0
    u = proj(off, pool_w); off += pool_w
    b_gate = proj(off, conv_w); off += conv_w
    c_gate = proj(off, conv_w); off += conv_w
    h_conv = proj(off, conv_w); off += conv_w
    pext_ref[:, POOL_HIST:, :] = u.reshape(nb, rows, pool_w)
    cext_ref[:, CONV_HIST:, :] = (c_gate * h_conv).reshape(nb, rows, conv_w)

    group = pool_w // len(POOL_WINDOWS)
    lane = lax.broadcasted_iota(jnp.int32, (1, pool_w), 1)
    pos = pos0 + i * rows + lax.broadcasted_iota(jnp.int32, (rows, 1), 0)
    cnt = [jnp.minimum(w, pos + 1).astype(F32) for w in POOL_WINDOWS]
    count = jnp.where(lane < group, cnt[0],
                      jnp.where(lane < 2 * group, cnt[1],
                                jnp.where(lane < 3 * group, cnt[2], cnt[3])))
    for n in range(nb):
        e = pext_ref[n]
        s2 = e + pltpu.roll(e, 1, 0)
        s4 = s2 + pltpu.roll(s2, 2, 0)
        s8 = s4 + pltpu.roll(s4, 4, 0)
        s16 = s8 + pltpu.roll(s8, 8, 0)
        win = jnp.where(lane < group, s2,
                        jnp.where(lane < 2 * group, s4,
                                  jnp.where(lane < 3 * group, s8, s16)))[POOL_HIST:]
        diff = (win / count - e[POOL_HIST:]).astype(BF16)
        pooled = jnp.dot(diff, pw_ref[...], preferred_element_type=F32) * ps_ref[...]
        pc_ref[n, :, 0:pool_w] = pooled.astype(BF16)

        z = cext_ref[n]
        conv = cw_ref[2:3, :] * z + cw_ref[1:2, :] * pltpu.roll(z, 1, 0) + cw_ref[0:1, :] * pltpu.roll(z, 2, 0)
        gated = b_gate[n * rows:(n + 1) * rows] * conv[CONV_HIST:]
        pc_ref[n, :, pool_w:pool_w + conv_w] = gated.astype(BF16)

    pt_ref[...] = pext_ref[:, rows:rows + POOL_HIST, :]
    ct_ref[...] = cext_ref[:, rows:rows + CONV_HIST, :]

    q = proj(off, attn_w); off += attn_w
    q_ref[...] = (q * (QK_DIM ** -0.5 * LOG2E)).astype(BF16).reshape(nb, rows, attn_w)
    for idx, (dense_ref, out_ref) in enumerate(((kb_ref, k_ref), (vb_ref, v_ref))):
        kv = proj(off, attn_w); off += attn_w
        dense_ref[...] = kv.astype(BF16).reshape(nb, rows, attn_w)
        if n_prev:
            out_ref[0:n_prev] = prev_refs[idx][...]
        for n in range(nb):
            for hd in range(heads):
                out_ref[n_prev, n, pl.ds(hd, rows, stride=heads), :] = kv[n * rows:(n + 1) * rows,
                                                                          hd * V_DIM:(hd + 1) * V_DIM]


def _inproj_call(x, w_in, pool_state, conv_state, pool_w_bd, pool_scale, conv_w, k_prev, v_prev, pos0, rows):
    bsz, s, d = x.shape
    nb = bsz if rows == s and bsz * s <= TOKEN_TILE else 1
    attn_w = (w_in.shape[1] - pool_state.shape[2] - 3 * conv_state.shape[2]) // 3
    heads = attn_w // V_DIM
    pool_w = pool_state.shape[2]
    conv_dim = conv_state.shape[2]
    n_prev = 0 if k_prev is None else k_prev.shape[0]

    def tile(width):
        return pl.BlockSpec((nb, rows, width), lambda b, i: (b, i, 0))

    def layers_spec(n):
        return pl.BlockSpec((n, nb, rows * heads, V_DIM), lambda b, i: (0, b, i, 0))

    in_specs = [tile(d), _const_spec(w_in.shape),
                pl.BlockSpec((nb, POOL_HIST, pool_w), lambda b, i: (b, 0, 0)),
                pl.BlockSpec((nb, CONV_HIST, conv_dim), lambda b, i: (b, 0, 0)),
                _const_spec(pool_w_bd.shape), _const_spec(pool_scale.shape), _const_spec(conv_w.shape)]
    args = [x, w_in, pool_state, conv_state, pool_w_bd, pool_scale, conv_w]
    if n_prev:
        in_specs += [layers_spec(n_prev), layers_spec(n_prev)]
        args += [k_prev, v_prev]
    kv_shape = jax.ShapeDtypeStruct((n_prev + 1, bsz, s * heads, V_DIM), F32)
    dense = jax.ShapeDtypeStruct((bsz, s, attn_w), BF16)
    out_shape = (dense, dense, dense, kv_shape, kv_shape,
                 jax.ShapeDtypeStruct((bsz, s, pool_w + conv_dim), BF16),
                 jax.ShapeDtypeStruct((bsz, POOL_HIST, pool_w), F32),
                 jax.ShapeDtypeStruct((bsz, CONV_HIST, conv_dim), F32))
    out_specs = (tile(attn_w), tile(attn_w), tile(attn_w), layers_spec(n_prev + 1), layers_spec(n_prev + 1),
                 tile(pool_w + conv_dim),
                 pl.BlockSpec((nb, POOL_HIST, pool_w), lambda b, i: (b, 0, 0)),
                 pl.BlockSpec((nb, CONV_HIST, conv_dim), lambda b, i: (b, 0, 0)))
    return pl.pallas_call(
        functools.partial(_inproj_kernel, pos0, n_prev),
        out_shape=out_shape,
        grid=(bsz // nb, s // rows),
        in_specs=in_specs,
        out_specs=out_specs,
        scratch_shapes=[pltpu.VMEM((nb, POOL_HIST + rows, pool_w), F32),
                        pltpu.VMEM((nb, CONV_HIST + rows, conv_dim), F32)],
        compiler_params=_params(("arbitrary", "arbitrary")),
        name="inproj",
    )(*args)


def _bias_kernel(q0, k0, ref_bucket, rb_ref, o_ref):
    h = pl.program_id(0)
    _, nr, nc = o_ref.shape
    qp = q0 + lax.broadcasted_iota(jnp.int32, (nr, nc), 0)
    kp = k0 + lax.broadcasted_iota(jnp.int32, (nr, nc), 1)
    rel = kp - qp
    n = jnp.abs(rel)
    nb = NUM_BUCKETS // 2
    max_exact = nb // 2
    steps = nb - max_exact
    large = jnp.full((nr, nc), max_exact, jnp.int32)
    for j in range(1, steps):
        thr = math.ceil(max_exact * (MAX_DISTANCE / max_exact) ** (j / steps) - 1e-9)
        large = large + (n >= thr).astype(jnp.int32)
    bucket = jnp.where(rel > 0, nb, 0) + jnp.where(n < max_exact, n, large)
    val = jnp.zeros((nr, nc), F32)
    for bkt in range(NUM_BUCKETS):
        val = jnp.where(bucket == bkt, rb_ref[bkt, h], val)
    val = (val - rb_ref[ref_bucket, h]) * LOG2E
    visible = (kp // CHUNK) <= (qp // CHUNK)
    o_ref[0] = jnp.where(visible, val, NEG_INF)


def _bias_call(rel_bias, q0, k0, nr, nc):
    heads = rel_bias.shape[1]
    ref_bucket = NUM_BUCKETS // 2 - 1
    return pl.pallas_call(
        functools.partial(_bias_kernel, q0, k0, ref_bucket),
        out_shape=jax.ShapeDtypeStruct((heads, nr, nc), F32),
        grid=(heads,),
        in_specs=[pl.BlockSpec(memory_space=pltpu.SMEM)],
        out_specs=pl.BlockSpec((1, nr, nc), lambda h: (h, 0, 0)),
        compiler_params=_params(("arbitrary",)),
        name="relbias",
    )(rel_bias)


def _lambda_value(dl_ref, lam_init):
    dl = dl_ref[...]
    a = jnp.sum(dl[0:1] * dl[1:2], axis=-1, keepdims=True)
    b = jnp.sum(dl[2:3] * dl[3:4], axis=-1, keepdims=True)
    return jnp.exp(a) - jnp.exp(b) + lam_init


def _split_components(q):
    lane = lax.broadcasted_iota(jnp.int32, q.shape, 1)
    zero = jnp.zeros_like(q)
    return jnp.where(lane < QK_DIM, q, zero), jnp.where(lane >= QK_DIM, q, zero)


def _scores(qm, kt):
    return lax.dot_general(qm, kt, (((1,), (1,)), ((), ())), preferred_element_type=F32)


def _sub_norm(o, g, lam_init):
    o = o * lax.rsqrt(jnp.mean(o * o, axis=-1, keepdims=True) + RMS_EPS) * g
    return o * (1.0 - lam_init)


def _attn_kernel(lam_init, q_ref, k_ref, v_ref, d0_ref, d1_ref, dl_ref, g_ref, o_ref, m_ref, l_ref, acc_ref):
    t = d0_ref.shape[1]
    per_step = q_ref.shape[1] // t
    lam = _lambda_value(dl_ref, lam_init)
    for sub in range(per_step):
        _attn_query_tile(pl.program_id(2) * per_step + sub, slice(sub * t, (sub + 1) * t), lam, lam_init,
                         q_ref, k_ref, v_ref, d0_ref, d1_ref, g_ref, o_ref, m_ref, l_ref, acc_ref)


def _attn_query_tile(i, rows, lam, lam_init, q_ref, k_ref, v_ref, d0_ref, d1_ref, g_ref, o_ref,
                     m_ref, l_ref, acc_ref):
    t = d0_ref.shape[1]
    qms = _split_components(q_ref[0, rows, :])
    m_ref[...] = jnp.full(m_ref.shape, NEG_INF, F32)
    l_ref[...] = jnp.zeros(l_ref.shape, F32)
    acc_ref[...] = jnp.zeros(acc_ref.shape, F32)

    def step(j, bias, tiles=1):
        off = pl.multiple_of(j * t, t)
        kt = k_ref[0, pl.ds(off, tiles * t), :]
        vt = v_ref[0, pl.ds(off, tiles * t), :]
        scores = [_scores(qm, kt) for qm in qms]
        if bias is not None:
            scores = [s + bias for s in scores]
        for c, s in enumerate(scores):
            blocks = _lane_blocks(s)
            m_prev = m_ref[c]
            m_cur = jnp.maximum(m_prev, jnp.max(_tree(jnp.maximum, blocks), axis=-1, keepdims=True))
            scale = jnp.exp2(m_prev - m_cur)
            ps = [jnp.exp2(blk - m_cur) for blk in blocks]
            l_ref[c] = scale * l_ref[c] + _tree(jnp.add, ps)
            p = jnp.concatenate([x.astype(BF16) for x in ps], axis=1)
            acc_ref[c] = scale * acc_ref[c] + jnp.dot(p, vt, preferred_element_type=F32)
            m_ref[c] = m_cur

    n_far = jnp.maximum(i - 1, 0)

    def far_quad(jj, carry):
        step(4 * jj, None, tiles=4)
        return carry

    lax.fori_loop(0, n_far // 4, far_quad, 0)

    @pl.when(n_far % 4 >= 2)
    def _():
        step(n_far // 4 * 4, None, tiles=2)

    @pl.when(n_far % 2 == 1)
    def _():
        step(n_far - 1, None)

    @pl.when(i > 0)
    def _():
        step(i - 1, jnp.concatenate([d1_ref[0], d0_ref[0]], axis=1), tiles=2)

    @pl.when(i == 0)
    def _():
        step(0, d0_ref[0])

    outs = [acc_ref[c] / jnp.sum(l_ref[c], axis=-1, keepdims=True) for c in range(2)]
    o_ref[0, rows, :] = _sub_norm(outs[0] - lam * outs[1], g_ref[...], lam_init).astype(BF16)


def _attn_call(q, k, v, d0, d1, dl, g, lam_init):
    bsz, s, width = q.shape
    heads = width // V_DIM
    t = min(ATTN_TILE, s)
    per_step = math.gcd(ATTN_TILES_PER_STEP, s // t)
    kv_spec = pl.BlockSpec((1, s, V_DIM), lambda b, h, i: (b, 0, h))
    tile_spec = pl.BlockSpec((1, per_step * t, V_DIM), lambda b, h, i: (b, i, h))
    return pl.pallas_call(
        functools.partial(_attn_kernel, lam_init),
        out_shape=jax.ShapeDtypeStruct((bsz, s, width), BF16),
        grid=(bsz, heads, s // (per_step * t)),
        in_specs=[tile_spec, kv_spec, kv_spec,
                  pl.BlockSpec((1, t, t), lambda b, h, i: (h, 0, 0)),
                  pl.BlockSpec((1, t, t), lambda b, h, i: (h, 0, 0)),
                  pl.BlockSpec(dl.shape, lambda b, h, i: (0, 0)),
                  pl.BlockSpec(g.shape, lambda b, h, i: (0, 0))],
        out_specs=tile_spec,
        scratch_shapes=[pltpu.VMEM((2, t, LANES), F32), pltpu.VMEM((2, t, LANES), F32),
                        pltpu.VMEM((2, t, V_DIM), F32)],
        compiler_params=_params(("arbitrary", "arbitrary", "arbitrary")),
        name="attn_prompt",
    )(q, k, v, d0, d1, dl, g)


def _attn_cache_kernel(lam_init, heads, q_ref, ck_ref, cv_ref, nk_ref, nv_ref, bn_ref, bw_ref, dl_ref, g_ref,
                       o_ref):
    past = ck_ref.shape[2] // heads
    far_n = past - NEAR_WINDOW
    lam = _lambda_value(dl_ref, lam_init)
    for hd in range(heads):
        lanes = slice(hd * V_DIM, (hd + 1) * V_DIM)
        kb = ck_ref[0, 0, pl.ds(hd, past, stride=heads), :].astype(BF16)
        vb = cv_ref[0, 0, pl.ds(hd, past, stride=heads), :].astype(BF16)
        nkb = nk_ref[0, :, lanes]
        nvb = nv_ref[0, :, lanes]
        qm = jnp.concatenate(_split_components(q_ref[0, :, lanes]), axis=0)
        n_q = qm.shape[0] // 2
        s_far = _scores(qm, kb[:far_n])
        s_near = _scores(qm, kb[far_n:]) + jnp.concatenate([bn_ref[hd]] * 2, axis=0)
        s_new = _scores(qm, nkb) + jnp.concatenate([bw_ref[hd]] * 2, axis=0)
        m = jnp.maximum(jnp.max(s_far, axis=-1, keepdims=True),
                        jnp.maximum(jnp.max(s_near, axis=-1, keepdims=True),
                                    jnp.max(s_new, axis=-1, keepdims=True)))
        p_far = jnp.exp2(s_far - m)
        p_near = jnp.exp2(s_near - m)
        p_new = jnp.exp2(s_new - m)
        denom = (jnp.sum(p_far, axis=-1, keepdims=True) + jnp.sum(p_near, axis=-1, keepdims=True)
                 + jnp.sum(p_new, axis=-1, keepdims=True))
        acc = jnp.dot(p_far.astype(BF16), vb[:far_n], preferred_element_type=F32)
        acc = acc + jnp.dot(p_near.astype(BF16), vb[far_n:], preferred_element_type=F32)
        acc = acc + jnp.dot(p_new.astype(BF16), nvb, preferred_element_type=F32)
        out = acc / denom
        o = out[:n_q] - lam * out[n_q:]
        o_ref[0, :, lanes] = _sub_norm(o, g_ref[...], lam_init).astype(BF16)


def _attn_cache_call(q, cache_k, cache_v, k_new, v_new, b_near, b_new, dl, g, layer, lam_init):
    bsz, s, width = q.shape
    heads = width // V_DIM
    cache_spec = pl.BlockSpec((1, 1) + cache_k.shape[2:], lambda b: (layer, b, 0, 0))
    tile_spec = pl.BlockSpec((1, s, width), lambda b: (b, 0, 0))
    return pl.pallas_call(
        functools.partial(_attn_cache_kernel, lam_init, heads),
        out_shape=jax.ShapeDtypeStruct((bsz, s, width), BF16),
        grid=(bsz,),
        in_specs=[tile_spec, cache_spec, cache_spec, tile_spec, tile_spec,
                  _const_spec(b_near.shape), _const_spec(b_new.shape), _const_spec(dl.shape), _const_spec(g.shape)],
        out_specs=tile_spec,
        compiler_params=_params(("arbitrary",)),
        name="attn_sample",
    )(q, cache_k, cache_v, k_new, v_new, b_near, b_new, dl, g)


def _block_diag(w):
    groups, c, _ = w.shape
    eye = jnp.eye(groups, dtype=w.dtype)
    return (eye[:, None, :, None] * w[:, :, None, :]).reshape(groups * c, groups * c)


def _pad_front(state, rows):
    return jnp.pad(state, ((0, 0), (rows - state.shape[1], 0), (0, 0)))


def _layer(x, lw, alpha, pool_state, conv_state, k_all, v_all, pos0, attend):
    bsz, s, d = x.shape
    x1 = _ffn_call(x.reshape(bsz * s, d), lw["wi1"], lw["wo1"], lw["g0"], lw["b0"], alpha)
    rows = min(TOKEN_TILE, s)
    q, kb, vb, k_all, v_all, pc, ptail, ctail = _inproj_call(
        x1.reshape(bsz, s, d), lw["w_in"], pool_state, conv_state, lw["pool_w"], lw["pool_scale"], lw["conv_w"],
        k_all, v_all, pos0, rows)
    attn = attend(q, kb, vb)
    x3 = _mixffn_call(x1, pc.reshape(bsz * s, -1), attn.reshape(bsz * s, -1), lw["w_out"], lw["g1"], lw["b1"],
                      lw["wi2"], lw["wo2"], lw["g2"], lw["b2"], alpha)
    return x3.reshape(bsz, s, d), k_all, v_all, ptail, ctail


def kernel(x_prompt, x_sample, cache_k, cache_v, state_pool, state_conv, ln_g, ln_b, w_ffn_in, w_ffn_out,
           w_in, w_out, pool_w, pool_scale, conv_w, diff_lambda, subln_g, rel_bias):
    depth = w_in.shape[0]
    bp, sp, d = x_prompt.shape
    bs, ss, _ = x_sample.shape
    past = cache_k.shape[2]
    heads = cache_k.shape[3]
    alpha = (2 * depth) ** 0.25
    t = min(ATTN_TILE, sp)

    d0 = _bias_call(rel_bias, 0, 0, t, t)
    d1 = _bias_call(rel_bias, t, 0, t, t)
    b_near = _bias_call(rel_bias, past, past - NEAR_WINDOW, ss, NEAR_WINDOW)
    b_new = _bias_call(rel_bias, past, past, ss, ss)

    ck = cache_k.reshape(depth, bs, past * heads, V_DIM)
    cv = cache_v.reshape(depth, bs, past * heads, V_DIM)
    zeros_pool = jnp.zeros((bp, POOL_HIST, state_pool.shape[-1]), F32)
    zeros_conv = jnp.zeros((bp, CONV_HIST, state_conv.shape[-1]), F32)

    hp, hs = x_prompt, x_sample
    kp = vp = ks = vs = None
    tails = {"pp": [], "cp": [], "ps": [], "cs": []}
    for l in range(depth):
        lam_init = 0.8 - 0.6 * math.exp(-0.3 * l)
        lw = {
            "wi1": w_ffn_in[l, 0].astype(BF16), "wo1": w_ffn_out[l, 0].astype(BF16),
            "wi2": w_ffn_in[l, 1].astype(BF16), "wo2": w_ffn_out[l, 1].astype(BF16),
            "w_in": w_in[l].astype(BF16), "w_out": w_out[l].astype(BF16),
            "pool_w": _block_diag(pool_w[l]).astype(BF16),
            "pool_scale": pool_scale[l][None, :], "conv_w": conv_w[l],
            "g0": ln_g[l, 0][None, :], "b0": ln_b[l, 0][None, :],
            "g1": ln_g[l, 1][None, :], "b1": ln_b[l, 1][None, :],
            "g2": ln_g[l, 2][None, :], "b2": ln_b[l, 2][None, :],
        }
        dl = diff_lambda[l]
        g = subln_g[l][None, :]

        def attend_prompt(q, kb, vb, lam_init=lam_init, dl=dl, g=g):
            return _attn_call(q, kb, vb, d0, d1, dl, g, lam_init)

        def attend_sample(q, kb, vb, l=l, lam_init=lam_init, dl=dl, g=g):
            return _attn_cache_call(q, ck, cv, kb, vb, b_near, b_new, dl, g, l, lam_init)

        hp, kp, vp, pt, ct = _layer(hp, lw, alpha, zeros_pool, zeros_conv, kp, vp, 0, attend_prompt)
        tails["pp"].append(pt)
        tails["cp"].append(ct)
        hs, ks, vs, pt, ct = _layer(hs, lw, alpha, _pad_front(state_pool[l], POOL_HIST),
                                    _pad_front(state_conv[l], CONV_HIST), ks, vs, past, attend_sample)
        tails["ps"].append(pt)
        tails["cs"].append(ct)

    def states(parts, keep):
        return jnp.stack(parts)[:, :, -keep:, :]

    return (hp, hs,
            kp.reshape(depth, bp, sp, heads, V_DIM), vp.reshape(depth, bp, sp, heads, V_DIM),
            states(tails["pp"], POOL_STATE), states(tails["cp"], CONV_K - 1),
            ks.reshape(depth, bs, ss, heads, V_DIM), vs.reshape(depth, bs, ss, heads, V_DIM),
            states(tails["ps"], POOL_STATE), states(tails["cs"], CONV_K - 1))
```
